```python
import math
import jax, jax.numpy as jnp
from jax import lax
import numpy as np

D_MODEL = 1024
BATCH = 8
SEQ = 2048
DEPTH = 4
DEC_BATCH = 128
DEC_SEQ = 8
PAST_LEN = 16384
PAGE_SIZE = 128

MIX_WIDTH = D_MODEL
CONV_DIM = MIX_WIDTH // 2
CONV_GROUPS = 8
CONV_W = 3
HGRN_DIM = MIX_WIDTH - CONV_DIM
HGRN_HEAD_DIM = 128
HGRN_HEADS = HGRN_DIM // HGRN_HEAD_DIM
HGRN_CHUNK = 64
IN_PROJ_DIM = 3 * CONV_DIM + 4 * HGRN_DIM
N_GROUPS = 4
EXPERTS_PER_GROUP = 8
N_EXPERTS = N_GROUPS * EXPERTS_PER_GROUP
TOP_K = 2
D_EXPERT = D_MODEL // 4
MOE_BLOCK = 128
LN_EPS = 1e-5
RMS_EPS = 1e-6
DEEPNORM_ALPHA = (2 * DEPTH) ** 0.25
DEEPNORM_BETA = (8 * DEPTH) ** -0.25

kernel_name = "hymba_conv_hgrn2_hmoe_deepnorm_step"


def _layer_norm(x, g, b):
    xf = x.astype(jnp.float32)
    mu = jnp.mean(xf, axis=-1, keepdims=True)
    var = jnp.mean(jnp.square(xf - mu), axis=-1, keepdims=True)
    return ((xf - mu) * lax.rsqrt(var + LN_EPS)).astype(x.dtype) * g + b


def _short_conv_mixer(xb, xc, xv, buf, conv_w):
    L = xv.shape[1]
    v = xc * xv
    vp = jnp.concatenate([buf.astype(v.dtype), v], axis=1)
    y = sum(conv_w[j] * vp[:, j:j + L] for j in range(CONV_W))
    return xb * y, vp[:, L:]


def _gla_chunked(q, k, v, log_f, s0):
    B, L, H, K = q.shape
    V = v.shape[-1]
    C = math.gcd(L, HGRN_CHUNK)
    n = L // C

    def to_chunks(t):
        return t.reshape(B, n, C, H, t.shape[-1]).transpose(1, 0, 3, 2, 4)

    causal = jnp.tril(jnp.ones((C, C), dtype=bool))[:, :, None]

    def step(s, inp):
        qc, kc, vc, gc = inp
        b = jnp.cumsum(gc, axis=2)
        diff = b[:, :, :, None, :] - b[:, :, None, :, :]
        decay = jnp.where(causal, jnp.exp(jnp.where(causal, diff, 0.0)), 0.0)
        a = jnp.einsum("bhtk,bhtsk,bhsk->bhts", qc, decay, kc)
        o = jnp.einsum("bhts,bhsv->bhtv", a, vc) + jnp.einsum("bhtk,bhkv->bhtv", qc * jnp.exp(b), s)
        b_end = b[:, :, -1:, :]
        s = jnp.exp(b_end[:, :, 0, :, None]) * s + jnp.einsum("bhsk,bhsv->bhkv", kc * jnp.exp(b_end - b), vc)
        return s, o

    s, o = lax.scan(step, s0, (to_chunks(q), to_chunks(k), to_chunks(v), to_chunks(log_f)))
    return o.transpose(1, 0, 3, 2, 4).reshape(B, L, H, V), s


def _hgrn2_mixer(q, f, i, g, s0, lb, gnorm_w):
    B, L, _ = q.shape
    H, K = HGRN_HEADS, HGRN_HEAD_DIM
    qh = (jax.nn.silu(q.astype(jnp.float32)) * K ** -0.5).reshape(B, L, H, K)
    zf = f.astype(jnp.float32).reshape(B, L, H, K)
    lbh = lb.astype(jnp.float32).reshape(H, K)
    f_gate = lbh + (1.0 - lbh) * jax.nn.sigmoid(zf)
    log_f = jnp.log(f_gate)
    kh = (1.0 - lbh) * jax.nn.sigmoid(-zf)
    vh = i.astype(jnp.float32).reshape(B, L, H, K)
    o, s = _gla_chunked(qh, kh, vh, log_f, s0.astype(jnp.float32))
    o = o * lax.rsqrt(jnp.mean(jnp.square(o), axis=-1, keepdims=True) + RMS_EPS)
    o = o * gnorm_w.astype(jnp.float32) * jax.nn.silu(g.astype(jnp.float32)).reshape(B, L, H, K)
    return o.reshape(B, L, HGRN_DIM).astype(q.dtype), s.astype(s0.dtype)


def _grouped_experts(xt, eid, wts, w_gate, w_up, w_down):
    T, D = xt.shape
    S = T * TOP_K
    e = eid.reshape(-1)
    tok = jnp.arange(S, dtype=jnp.int32) // TOP_K
    order = jnp.argsort(e)
    e_sorted = e[order]
    counts = jnp.zeros((N_EXPERTS,), jnp.int32).at[e].add(1)
    padded = (counts + MOE_BLOCK - 1) // MOE_BLOCK * MOE_BLOCK
    start = jnp.cumsum(counts) - counts
    pad_end = jnp.cumsum(padded)
    pad_start = pad_end - padded
    dest = pad_start[e_sorted] + jnp.arange(S, dtype=jnp.int32) - start[e_sorted]
    n_blk = -(-(S + N_EXPERTS * (MOE_BLOCK - 1)) // MOE_BLOCK)
    P = n_blk * MOE_BLOCK
    buf_tok = jnp.full((P,), T, jnp.int32).at[dest].set(tok[order])
    xb = jnp.concatenate([xt, jnp.zeros((1, D), xt.dtype)], axis=0)[buf_tok].reshape(n_blk, MOE_BLOCK, D)
    blk_start = jnp.arange(n_blk, dtype=jnp.int32) * MOE_BLOCK
    blk_e = jnp.minimum(jnp.sum(pad_end[None, :] <= blk_start[:, None], axis=1), N_EXPERTS - 1)

    def expert_block(args):
        xblk, ex = args
        h = jax.nn.silu(xblk @ w_gate[ex]) * (xblk @ w_up[ex])
        return h @ w_down[ex]

    yb = lax.map(expert_block, (xb, blk_e)).reshape(P, D)
    y_slot = jnp.zeros((S, D), yb.dtype).at[order].set(yb[dest])
    return jnp.einsum("tkd,tk->td", y_slot.reshape(T, TOP_K, D), wts.astype(y_slot.dtype))


def _hier_moe(u, w_rg, w_re, w_e_gate, w_e_up, w_e_down):
    B, L, D = u.shape
    xt = u.reshape(B * L, D)
    T = xt.shape[0]
    p_group = jax.nn.softmax((xt @ w_rg).astype(jnp.float32), axis=-1)
    grp = jnp.argmax(p_group, axis=-1).astype(jnp.int32)
    p_sel = jnp.take_along_axis(p_group, grp[:, None], axis=-1)
    logit_e = (xt @ w_re).astype(jnp.float32).reshape(T, N_GROUPS, EXPERTS_PER_GROUP)
    le = jnp.take_along_axis(logit_e, grp[:, None, None], axis=1)[:, 0]
    top_p, top_i = lax.top_k(jax.nn.softmax(le, axis=-1), TOP_K)
    wts = p_sel * top_p / jnp.sum(top_p, axis=-1, keepdims=True)
    eid = grp[:, None] * EXPERTS_PER_GROUP + top_i.astype(jnp.int32)
    return _grouped_experts(xt, eid, wts, w_e_gate, w_e_up, w_e_down).reshape(B, L, D)


def _trunk_layer(x, mod, conv_buf, s0, lb, w_in, conv_w, gnorm_w, w_out, ln_g, ln_b,
                 w_rg, w_re, w_e_gate, w_e_up, w_e_down):
    shift1, scale1, gate1, shift2, scale2, gate2 = jnp.split(mod[:, None, :], 6, axis=-1)
    u = x * (1.0 + scale1) + shift1
    p = u @ w_in
    cuts = [CONV_DIM, 2 * CONV_DIM, 3 * CONV_DIM, 3 * CONV_DIM + HGRN_DIM,
            3 * CONV_DIM + 2 * HGRN_DIM, 3 * CONV_DIM + 3 * HGRN_DIM]
    cb, cc, cv, q, f, i, g = jnp.split(p, cuts, axis=-1)
    y_conv, new_buf = _short_conv_mixer(cb, cc, cv, conv_buf, conv_w)
    y_rec, new_s = _hgrn2_mixer(q, f, i, g, s0, lb, gnorm_w)
    mix = jnp.concatenate([y_conv, y_rec], axis=-1) @ w_out
    x = _layer_norm(DEEPNORM_ALPHA * x + gate1 * mix, ln_g[0], ln_b[0])
    u = x * (1.0 + scale2) + shift2
    y_moe = _hier_moe(u, w_rg, w_re, w_e_gate, w_e_up, w_e_down)
    x = _layer_norm(DEEPNORM_ALPHA * x + gate2 * y_moe, ln_g[1], ln_b[1])
    return x, new_buf, new_s


def setup_inputs(seed: int = 0) -> dict:
    key = jax.random.key(seed)
    ks = jax.random.split(key, 20)
    f32 = jnp.float32
    nrm = lambda k, shape, s: jax.random.normal(k, shape, f32) * s
    return {
        "x_prompt": nrm(ks[0], (BATCH, SEQ, D_MODEL), 1.0),
        "x_sample": nrm(ks[1], (DEC_BATCH, DEC_SEQ, D_MODEL), 1.0),
        "state_conv": nrm(ks[2], (DEPTH, DEC_BATCH, CONV_W - 1, CONV_DIM), 1.0),
        "state_hgrn": nrm(ks[3], (DEPTH, DEC_BATCH, HGRN_HEADS, HGRN_HEAD_DIM, HGRN_HEAD_DIM), 0.5),
        "c_prompt": nrm(ks[4], (BATCH, D_MODEL), 1.0),
        "c_sample": nrm(ks[5], (DEC_BATCH, D_MODEL), 1.0),
        "hgrn_lb_logits": nrm(ks[6], (DEPTH, HGRN_DIM), 0.5),
        "w_mod": nrm(ks[7], (DEPTH, D_MODEL, 6 * D_MODEL), 0.5 * D_MODEL ** -0.5),
        "b_mod": nrm(ks[8], (DEPTH, 6 * D_MODEL), 0.02),
        "w_in": nrm(ks[9], (DEPTH, D_MODEL, IN_PROJ_DIM), D_MODEL ** -0.5),
        "conv_w": nrm(ks[10], (DEPTH, CONV_W, CONV_DIM), CONV_W ** -0.5),
        "gnorm_w": 1.0 + nrm(ks[11], (DEPTH, HGRN_HEAD_DIM), 0.02),
        "w_out": nrm(ks[12], (DEPTH, MIX_WIDTH, D_MODEL), DEEPNORM_BETA * MIX_WIDTH ** -0.5),
        "ln_g": 1.0 + nrm(ks[13], (DEPTH, 2, D_MODEL), 0.02),
        "ln_b": nrm(ks[14], (DEPTH, 2, D_MODEL), 0.02),
        "w_router_group": nrm(ks[15], (DEPTH, D_MODEL, N_GROUPS), D_MODEL ** -0.5),
        "w_router_expert": nrm(ks[16], (DEPTH, D_MODEL, N_EXPERTS), D_MODEL ** -0.5),
        "w_e_gate": nrm(ks[17], (DEPTH, N_EXPERTS, D_MODEL, D_EXPERT), D_MODEL ** -0.5),
        "w_e_up": nrm(ks[18], (DEPTH, N_EXPERTS, D_MODEL, D_EXPERT), D_MODEL ** -0.5),
        "w_e_down": nrm(ks[19], (DEPTH, N_EXPERTS, D_EXPERT, D_MODEL), DEEPNORM_BETA * D_EXPERT ** -0.5),
    }


def reference(x_prompt, x_sample, state_conv, state_hgrn, c_prompt, c_sample, hgrn_lb_logits,
              w_mod, b_mod, w_in, conv_w, gnorm_w, w_out, ln_g, ln_b,
              w_router_group, w_router_expert, w_e_gate, w_e_up, w_e_down):
    lb_p = jax.nn.softmax(hgrn_lb_logits.astype(jnp.float32), axis=0)
    lower_bounds = jnp.cumsum(lb_p, axis=0) - lb_p[0:1]
    yp, ys = x_prompt, x_sample
    conv_p, hgrn_p, conv_s, hgrn_s = [], [], [], []
    for l in range(DEPTH):
        params = (w_in[l], conv_w[l], gnorm_w[l], w_out[l], ln_g[l], ln_b[l],
                  w_router_group[l], w_router_expert[l], w_e_gate[l], w_e_up[l], w_e_down[l])
        mod_p = jax.nn.silu(c_prompt) @ w_mod[l] + b_mod[l]
        mod_s = jax.nn.silu(c_sample) @ w_mod[l] + b_mod[l]
        zb = jnp.zeros((yp.shape[0], CONV_W - 1, CONV_DIM), yp.dtype)
        zs = jnp.zeros((yp.shape[0], HGRN_HEADS, HGRN_HEAD_DIM, HGRN_HEAD_DIM), yp.dtype)
        yp, cb, sb = _trunk_layer(yp, mod_p, zb, zs, lower_bounds[l], *params)
        ys, cs, ss = _trunk_layer(ys, mod_s, state_conv[l], state_hgrn[l], lower_bounds[l], *params)
        conv_p.append(cb)
        hgrn_p.append(sb)
        conv_s.append(cs)
        hgrn_s.append(ss)
    return (yp, ys, jnp.stack(conv_p), jnp.stack(hgrn_p), jnp.stack(conv_s), jnp.stack(hgrn_s))
```

```python
import functools

import jax
import jax.numpy as jnp
from jax import lax
from jax.experimental import pallas as pl
from jax.experimental.pallas import tpu as pltpu

F32 = jnp.float32
BF16 = jnp.bfloat16
I32 = jnp.int32

CONV_W = 3
HEAD_DIM = 128
N_GROUPS = 4
EXPERTS_PER_GROUP = 8
N_EXPERTS = N_GROUPS * EXPERTS_PER_GROUP
LN_EPS = 1e-5
RMS_EPS = 1e-6
LANES = 128
SUBLANES = 8
MOE_BLOCK = 256
VMEM_LIMIT = 56 * 1024 * 1024
NEG = -1e30


def _sigmoid(x):
    return 1.0 / (1.0 + jnp.exp(-x))


def _seg_row(x, seg, row):
    r, n = x.shape
    x3 = x.reshape(r // seg, seg, n)
    return jnp.broadcast_to(x3[:, row:row + 1, :], x3.shape).reshape(r, n)


def _dot_nt(a, b):
    return lax.dot_general(a, b, (((1,), (1,)), ((), ())), preferred_element_type=F32)


def _dot_tn(a, b):
    return lax.dot_general(a, b, (((0,), (0,)), ((), ())), preferred_element_type=F32)


def _mod_kernel(c_ref, w_ref, b_ref, o_ref):
    c = c_ref[...]
    a = (c * _sigmoid(c)).astype(BF16)
    o_ref[0] = jnp.dot(a, w_ref[0].astype(BF16), preferred_element_type=F32) + b_ref[0]


def _modulation(c_all, w_mod, b_mod):
    depth, d, d6 = w_mod.shape
    nb = c_all.shape[0]
    n_col = d6 // d
    return pl.pallas_call(
        _mod_kernel,
        grid=(depth, n_col),
        in_specs=[
            pl.BlockSpec((nb, d), lambda l, j: (0, 0)),
            pl.BlockSpec((1, d, d), lambda l, j: (l, 0, j)),
            pl.BlockSpec((1, 1, d), lambda l, j: (l, 0, j)),
        ],
        out_specs=pl.BlockSpec((1, nb, d), lambda l, j: (l, 0, j)),
        out_shape=jax.ShapeDtypeStruct((depth, nb, d6), F32),
        compiler_params=pltpu.CompilerParams(
            dimension_semantics=("arbitrary", "arbitrary"), vmem_limit_bytes=VMEM_LIMIT),
        name="modulation",
    )(c_all, w_mod, b_mod.reshape(depth, 1, d6))


def _mixer_kernel(*refs, layer, depth, bt, tl, chunk, has_state, alpha):
    if has_state:
        (x_ref, mod_ref, lbl_ref, w_in_ref, cw_ref, gn_ref, w_out_ref, lng_ref, lnb_ref, wr_ref,
         conv0_ref, s0_ref,
         x1_ref, u2_ref, ri_ref, rf_ref, conv_out_ref, s_out_ref, cnt_ref,
         ymix_ref, s_run_ref, tail_ref, cnt_run_ref) = refs
    else:
        (x_ref, mod_ref, lbl_ref, w_in_ref, cw_ref, gn_ref, w_out_ref, lng_ref, lnb_ref, wr_ref,
         x1_ref, u2_ref, ri_ref, rf_ref, conv_out_ref, s_out_ref, cnt_ref,
         ymix_ref, s_run_ref, tail_ref, cnt_run_ref) = refs
    d = x_ref.shape[-1]
    conv_dim = d // 2
    hg = d - conv_dim
    heads = hg // HEAD_DIM
    r = bt * tl
    n_units = r // chunk
    ib = pl.program_id(0)
    it = pl.program_id(1)
    first_tile = it == 0

    @pl.when(jnp.logical_and(ib == 0, first_tile))
    def _():
        cnt_run_ref[...] = jnp.zeros_like(cnt_run_ref)

    if not has_state:
        @pl.when(first_tile)
        def _():
            s_run_ref[...] = jnp.zeros_like(s_run_ref)
            tail_ref[...] = jnp.zeros_like(tail_ref)

    lbl = lbl_ref[...]
    lbe = jnp.exp(lbl - jnp.max(lbl, axis=0, keepdims=True))
    lbp = lbe / jnp.sum(lbe, axis=0, keepdims=True)
    lb_all = lbp[0:1] * 0.0
    for j in range(1, layer + 1):
        lb_all = lb_all + lbp[j:j + 1]

    x3 = x_ref[...]
    mod3 = mod_ref[...]
    shift1, scale1, gate1 = (mod3[:, :, k * d:(k + 1) * d] for k in range(3))
    shift2, scale2, gate2 = (mod3[:, :, k * d:(k + 1) * d] for k in range(3, 6))
    u = (x3 * (1.0 + scale1) + shift1).reshape(r, d).astype(BF16)

    def proj(k, width):
        return jnp.dot(u, w_in_ref[0, :, k:k + width], preferred_element_type=F32)

    cb = proj(0, conv_dim)
    cc = proj(conv_dim, conv_dim)
    cv = proj(2 * conv_dim, conv_dim)
    v = cc * cv
    if has_state:
        prev = conv0_ref[0]
    else:
        prev = tail_ref[...]
    rows_c = lax.broadcasted_iota(I32, (r, conv_dim), 0)
    pos_t = rows_c & (tl - 1)
    prev1 = jnp.broadcast_to(prev[:, 1:2, :], (bt, tl, conv_dim)).reshape(r, conv_dim)
    prev0 = jnp.broadcast_to(prev[:, 0:1, :], (bt, tl, conv_dim)).reshape(r, conv_dim)
    v1 = jnp.where(pos_t >= 1, pltpu.roll(v, 1, 0), prev1)
    v2 = jnp.where(pos_t >= 2, pltpu.roll(v, 2, 0), jnp.where(pos_t == 1, prev1, prev0))
    cw = cw_ref[0]
    y_conv = cb * (cw[0:1] * v2 + cw[1:2] * v1 + cw[2:3] * v)
    ymix_ref[:, 0:conv_dim] = y_conv.astype(BF16)
    new_tail = v.reshape(bt, tl, conv_dim)[:, tl - 2:tl, :]
    conv_out_ref[...] = new_tail
    if not has_state:
        tail_ref[...] = new_tail

    rows = lax.broadcasted_iota(I32, (r, HEAD_DIM), 0)
    pos8 = rows & (SUBLANES - 1)
    pos_c = rows & (chunk - 1)
    gn = gn_ref[0]
    t_i = lax.broadcasted_iota(I32, (chunk, chunk), 0)
    s_i = lax.broadcasted_iota(I32, (chunk, chunk), 1)
    q_off = 3 * conv_dim
    for h in range(heads):
        lo = h * HEAD_DIM
        lb = lb_all[:, lo:lo + HEAD_DIM]
        qz = proj(q_off + lo, HEAD_DIM)
        fz = proj(q_off + hg + lo, HEAD_DIM)
        vv = proj(q_off + 2 * hg + lo, HEAD_DIM)
        gz = proj(q_off + 3 * hg + lo, HEAD_DIM)
        qq = qz * _sigmoid(qz) * (HEAD_DIM ** -0.5)
        lf = jnp.log(lb + (1.0 - lb) * _sigmoid(fz))
        kk = (1.0 - lb) * _sigmoid(-fz)
        b = lf
        step = 1
        while step < chunk:
            b = b + jnp.where(pos_c >= step, pltpu.roll(b, step, 0), 0.0)
            step *= 2
        o = jnp.sum(qq * kk, axis=-1, keepdims=True) * vv
        for dd in range(1, SUBLANES):
            bd = pltpu.roll(b, dd, 0)
            kd = pltpu.roll(kk, dd, 0)
            vd = pltpu.roll(vv, dd, 0)
            e = qq * kd * jnp.exp(jnp.minimum(b - bd, 0.0))
            a = jnp.sum(e, axis=-1, keepdims=True)
            o = o + jnp.where(pos8 >= dd, a * vd, 0.0)
        vv16 = vv.astype(BF16)
        o_units = [None] * n_units
        if chunk > SUBLANES:
            a_units = [jnp.zeros((chunk, chunk), F32) for _ in range(n_units)]
            m = SUBLANES
            while m < chunk:
                br = _seg_row(b, 2 * m, m - 1)
                upper = (rows & (2 * m - 1)) >= m
                y = jnp.exp(jnp.minimum(jnp.where(upper, b - br, br - b), 0.0))
                qm = (qq * y).astype(BF16)
                km = (kk * y).astype(BF16)
                same = (t_i // (2 * m)) == (s_i // (2 * m))
                mask = jnp.where(same, jnp.where((t_i & (2 * m - 1)) >= m,
                                                 jnp.where((s_i & (2 * m - 1)) < m, 1.0, 0.0), 0.0), 0.0)
                for c in range(n_units):
                    sl = slice(c * chunk, (c + 1) * chunk)
                    a_units[c] = a_units[c] + mask * _dot_nt(qm[sl], km[sl])
                m *= 2
            for c in range(n_units):
                sl = slice(c * chunk, (c + 1) * chunk)
                o_units[c] = jnp.dot(a_units[c].astype(BF16), vv16[sl], preferred_element_type=F32)
        bend = _seg_row(b, chunk, chunk - 1)
        qt = (qq * jnp.exp(b)).astype(BF16)
        kh = (kk * jnp.exp(bend - b)).astype(BF16)
        bl = b.reshape(n_units, chunk, HEAD_DIM)[:, chunk - 1, :]
        bl = jnp.concatenate([bl, jnp.zeros((HEAD_DIM - n_units, HEAD_DIM), F32)], axis=0)
        dec_t = jnp.exp(bl.T)
        if not has_state:
            s_cur = s_run_ref[h]
        for c in range(n_units):
            sl = slice(c * chunk, (c + 1) * chunk)
            if has_state:
                s_cur = s0_ref[0, c, h]
            o_s = jnp.dot(qt[sl], s_cur.astype(BF16), preferred_element_type=F32)
            o_units[c] = o_s if o_units[c] is None else o_units[c] + o_s
            s_new = dec_t[:, c:c + 1] * s_cur + _dot_tn(kh[sl], vv16[sl])
            if has_state:
                s_out_ref[c, h] = s_new
            else:
                s_cur = s_new
        if not has_state:
            s_run_ref[h] = s_cur
            s_out_ref[0, h] = s_cur
        o = o + jnp.concatenate(o_units, axis=0)
        o = o * lax.rsqrt(jnp.mean(o * o, axis=-1, keepdims=True) + RMS_EPS)
        o = o * gn * (gz * _sigmoid(gz))
        ymix_ref[:, conv_dim + lo:conv_dim + lo + HEAD_DIM] = o.astype(BF16)

    mix = jnp.dot(ymix_ref[...], w_out_ref[0], preferred_element_type=F32).reshape(bt, tl, d)
    z = alpha * x3 + gate1 * mix
    mu = jnp.mean(z, axis=-1, keepdims=True)
    zc = z - mu
    var = jnp.mean(zc * zc, axis=-1, keepdims=True)
    x1 = zc * lax.rsqrt(var + LN_EPS) * lng_ref[0, 0:1, :] + lnb_ref[0, 0:1, :]
    x1_ref[...] = x1
    u2 = (x1 * (1.0 + scale2) + shift2).reshape(r, d)
    u2_ref[...] = u2

    logits = jnp.dot(u2, wr_ref[0], preferred_element_type=F32, precision=lax.Precision.HIGHEST)
    lane = lax.broadcasted_iota(I32, (r, LANES), 1)
    is_g = jnp.logical_and(lane >= N_EXPERTS, lane < N_EXPERTS + N_GROUPS)
    gl = jnp.where(is_g, logits, NEG)
    gm = jnp.max(gl, axis=-1, keepdims=True)
    grp = jnp.min(jnp.where(gl == gm, lane - N_EXPERTS, LANES), axis=-1, keepdims=True)
    p_sel = 1.0 / jnp.sum(jnp.where(is_g, jnp.exp(gl - gm), 0.0), axis=-1, keepdims=True)
    in_grp = jnp.logical_and(lane < N_EXPERTS, (lane // EXPERTS_PER_GROUP) == grp)
    el = jnp.where(in_grp, logits, NEG)
    m1 = jnp.max(el, axis=-1, keepdims=True)
    i1 = jnp.min(jnp.where(jnp.logical_and(in_grp, el == m1), lane, LANES), axis=-1, keepdims=True)
    in2 = jnp.logical_and(in_grp, lane != i1)
    el2 = jnp.where(in2, logits, NEG)
    m2 = jnp.max(el2, axis=-1, keepdims=True)
    i2 = jnp.min(jnp.where(jnp.logical_and(in2, el2 == m2), lane, LANES), axis=-1, keepdims=True)
    e21 = jnp.exp(m2 - m1)
    w1 = p_sel / (1.0 + e21)
    w2 = p_sel * e21 / (1.0 + e21)
    oh1 = lane == i1
    oh2 = lane == i2
    ohs = jnp.where(oh1, 1.0, 0.0) + jnp.where(oh2, 1.0, 0.0)
    tr = lax.broadcasted_iota(I32, (r, r), 0)
    tc = lax.broadcasted_iota(I32, (r, r), 1)
    ltri = jnp.where(tr > tc, 1.0, 0.0).astype(BF16)
    cnt = cnt_run_ref[...]
    pref = jnp.dot(ltri, ohs.astype(BF16), preferred_element_type=F32) + cnt
    rank1 = jnp.sum(jnp.where(oh1, pref, 0.0), axis=-1, keepdims=True).astype(I32)
    rank2 = jnp.sum(jnp.where(oh2, pref, 0.0), axis=-1, keepdims=True).astype(I32)
    cnt = cnt + jnp.sum(ohs, axis=0, keepdims=True)
    cnt_run_ref[...] = cnt
    cnt_ref[...] = cnt.astype(I32)
    ri_ref[...] = jnp.where(lane == 0, i1, jnp.where(lane == 1, i2, jnp.where(
        lane == 2, rank1, jnp.where(lane == 3, rank2, 0))))
    rf_ref[...] = jnp.where(lane == 0, w1, jnp.where(lane == 1, w2, 0.0))


def _mixer(layer, x, mod, lb_logits, w_in16, conv_w, gnorm3, w_out16, ln_g, ln_b, w_r,
           state_conv, state_hgrn, *, bt, tl, chunk):
    nb, seq, d = x.shape
    depth = w_in16.shape[0]
    conv_dim = d // 2
    hg = d - conv_dim
    heads = hg // HEAD_DIM
    r = bt * tl
    has_state = state_conv is not None
    if has_state:
        assert seq == tl == chunk
    grid = (nb // bt, seq // tl)
    const = dict(pipeline_mode=pl.Buffered(1))
    in_specs = [
        pl.BlockSpec((bt, tl, d), lambda b, t: (b, t, 0)),
        pl.BlockSpec((bt, 1, 6 * d), lambda b, t: (b, 0, 0)),
        pl.BlockSpec(lb_logits.shape, lambda b, t: (0, 0), **const),
        pl.BlockSpec((1, d, w_in16.shape[2]), lambda b, t: (layer, 0, 0), **const),
        pl.BlockSpec((1, CONV_W, conv_dim), lambda b, t: (layer, 0, 0), **const),
        pl.BlockSpec((1, 1, HEAD_DIM), lambda b, t: (layer, 0, 0), **const),
        pl.BlockSpec((1, d, d), lambda b, t: (layer, 0, 0), **const),
        pl.BlockSpec((1, 2, d), lambda b, t: (layer, 0, 0), **const),
        pl.BlockSpec((1, 2, d), lambda b, t: (layer, 0, 0), **const),
        pl.BlockSpec((1, d, LANES), lambda b, t: (layer, 0, 0), **const),
    ]
    args = [x, mod, lb_logits, w_in16, conv_w, gnorm3, w_out16, ln_g, ln_b, w_r]
    if has_state:
        in_specs += [
            pl.BlockSpec((1, bt, CONV_W - 1, conv_dim), lambda b, t: (layer, b, 0, 0)),
            pl.BlockSpec((1, bt, heads, HEAD_DIM, HEAD_DIM), lambda b, t: (layer, b, 0, 0, 0)),
        ]
        args += [state_conv, state_hgrn]
    out_shape = (
        jax.ShapeDtypeStruct((nb, seq, d), F32),
        jax.ShapeDtypeStruct((nb * seq, d), F32),
        jax.ShapeDtypeStruct((nb * seq, LANES), I32),
        jax.ShapeDtypeStruct((nb * seq, LANES), F32),
        jax.ShapeDtypeStruct((nb, CONV_W - 1, conv_dim), F32),
        jax.ShapeDtypeStruct((nb, heads, HEAD_DIM, HEAD_DIM), F32),
        jax.ShapeDtypeStruct((1, LANES), I32),
    )
    n_t = seq // tl
    out_specs = (
        pl.BlockSpec((bt, tl, d), lambda b, t: (b, t, 0)),
        pl.BlockSpec((r, d), lambda b, t: (b * n_t + t, 0)),
        pl.BlockSpec((r, LANES), lambda b, t: (b * n_t + t, 0)),
        pl.BlockSpec((r, LANES), lambda b, t: (b * n_t + t, 0)),
        pl.BlockSpec((bt, CONV_W - 1, conv_dim), lambda b, t: (b, 0, 0)),
        pl.BlockSpec((bt, heads, HEAD_DIM, HEAD_DIM), lambda b, t: (b, 0, 0, 0)),
        pl.BlockSpec((1, LANES), lambda b, t: (0, 0)),
    )
    scratch = [
        pltpu.VMEM((r, d), BF16),
        pltpu.VMEM((heads, HEAD_DIM, HEAD_DIM), F32),
        pltpu.VMEM((1, CONV_W - 1, conv_dim), F32),
        pltpu.VMEM((1, LANES), F32),
    ]
    kern = functools.partial(_mixer_kernel, layer=layer, depth=depth, bt=bt, tl=tl, chunk=chunk,
                             has_state=has_state, alpha=float((2 * depth) ** 0.25))
    return pl.pallas_call(
        kern, grid=grid, in_specs=in_specs, out_specs=out_specs, out_shape=out_shape,
        scratch_shapes=scratch,
        compiler_params=pltpu.CompilerParams(
            dimension_semantics=("arbitrary", "arbitrary"), vmem_limit_bytes=VMEM_LIMIT),
        name="mixer_state" if has_state else "mixer_seq",
    )(*args)


def _scatter_kernel(pad_end_ref, dest_ref, u_ref, xs_ref, zero_ref, sem, zsem):
    n_tok = u_ref.shape[0]
    blk = zero_ref.shape[0]

    @pl.when(pl.program_id(0) == 0)
    def _():
        zero_ref[...] = jnp.zeros_like(zero_ref)

        def zcopy(e):
            end = pl.multiple_of(pad_end_ref[e], blk)
            return pltpu.make_async_copy(zero_ref, xs_ref.at[pl.ds(end - blk, blk)], zsem)

        def has_rows(e):
            prev_end = jnp.where(e == 0, 0, pad_end_ref[jnp.maximum(e - 1, 0)])
            return pad_end_ref[e] > prev_end

        def start(e, carry):
            @pl.when(has_rows(e))
            def _():
                zcopy(e).start()
            return carry

        def wait(e, carry):
            @pl.when(has_rows(e))
            def _():
                zcopy(e).wait()
            return carry

        lax.fori_loop(0, N_EXPERTS, start, 0)
        lax.fori_loop(0, N_EXPERTS, wait, 0)

    def row_copy(t, k):
        dst = dest_ref[2 * t + k]
        return pltpu.make_async_copy(u_ref.at[pl.ds(t, 1)], xs_ref.at[pl.ds(dst, 1)], sem)

    def start(t, carry):
        row_copy(t, 0).start()
        row_copy(t, 1).start()
        return carry

    def wait(t, carry):
        row_copy(t, 0).wait()
        row_copy(t, 1).wait()
        return carry

    lax.fori_loop(0, n_tok, start, 0)
    lax.fori_loop(0, n_tok, wait, 0)


def _scatter(u2, dest, pad_end, n_rows, tile):
    n_tok, d = u2.shape
    grid_spec = pltpu.PrefetchScalarGridSpec(
        num_scalar_prefetch=1,
        grid=(n_tok // tile,),
        in_specs=[
            pl.BlockSpec((2 * tile,), lambda i, pe: (i,), memory_space=pltpu.SMEM),
            pl.BlockSpec((tile, d), lambda i, pe: (i, 0)),
        ],
        out_specs=pl.BlockSpec(memory_space=pl.ANY),
        scratch_shapes=[
            pltpu.VMEM((MOE_BLOCK, d), F32),
            pltpu.SemaphoreType.DMA(()),
            pltpu.SemaphoreType.DMA(()),
        ],
    )
    return pl.pallas_call(
        _scatter_kernel, grid_spec=grid_spec,
        out_shape=jax.ShapeDtypeStruct((n_rows, d), F32),
        compiler_params=pltpu.CompilerParams(
            dimension_semantics=("arbitrary",), vmem_limit_bytes=VMEM_LIMIT, has_side_effects=True),
        name="moe_scatter",
    )(pad_end, dest, u2)


def _expert_kernel(blk_e_ref, n_used_ref, xs_ref, wg_ref, wu_ref, wd_ref, yb_ref):
    @pl.when(pl.program_id(0) < n_used_ref[0])
    def _():
        x = xs_ref[...].astype(BF16)
        g = jnp.dot(x, wg_ref[0, 0], preferred_element_type=F32)
        up = jnp.dot(x, wu_ref[0, 0], preferred_element_type=F32)
        hid = (g * _sigmoid(g) * up).astype(BF16)
        yb_ref[...] = jnp.dot(hid, wd_ref[0, 0], preferred_element_type=F32)


def _experts(layer, xs, blk_e, n_used, wg16, wu16, wd16):
    n_rows, d = xs.shape
    de = wg16.shape[-1]
    n_blk = n_rows // MOE_BLOCK

    def blk(j, be, nu):
        return jnp.minimum(j, nu[0] - 1)

    grid_spec = pltpu.PrefetchScalarGridSpec(
        num_scalar_prefetch=2,
        grid=(n_blk,),
        in_specs=[
            pl.BlockSpec((MOE_BLOCK, d), lambda j, be, nu: (blk(j, be, nu), 0)),
            pl.BlockSpec((1, 1, d, de), lambda j, be, nu: (layer, be[blk(j, be, nu)], 0, 0)),
            pl.BlockSpec((1, 1, d, de), lambda j, be, nu: (layer, be[blk(j, be, nu)], 0, 0)),
            pl.BlockSpec((1, 1, de, d), lambda j, be, nu: (layer, be[blk(j, be, nu)], 0, 0)),
        ],
        out_specs=pl.BlockSpec((MOE_BLOCK, d), lambda j, be, nu: (blk(j, be, nu), 0)),
    )
    return pl.pallas_call(
        _expert_kernel, grid_spec=grid_spec,
        out_shape=jax.ShapeDtypeStruct((n_rows, d), F32),
        compiler_params=pltpu.CompilerParams(
            dimension_semantics=("arbitrary",), vmem_limit_bytes=VMEM_LIMIT),
        name="moe_experts",
    )(blk_e, n_used, xs, wg16, wu16, wd16)


def _combine_kernel(dest_ref, yb_ref, x1_ref, rf_ref, mod_ref, lng_ref, lnb_ref, o_ref,
                    y0_ref, y1_ref, sem, *, alpha):
    bt, tl, d = x1_ref.shape
    r = bt * tl

    def row_copy(t, k):
        dst = y0_ref if k == 0 else y1_ref
        return pltpu.make_async_copy(yb_ref.at[pl.ds(dest_ref[2 * t + k], 1)], dst.at[pl.ds(t, 1)], sem)

    def start(t, carry):
        row_copy(t, 0).start()
        row_copy(t, 1).start()
        return carry

    def wait(t, carry):
        row_copy(t, 0).wait()
        row_copy(t, 1).wait()
        return carry

    lax.fori_loop(0, r, start, 0)
    lax.fori_loop(0, r, wait, 0)
    rf = rf_ref[...]
    y = rf[:, 0:1] * y0_ref[...] + rf[:, 1:2] * y1_ref[...]
    gate2 = mod_ref[:, :, 5 * d:6 * d]
    z = alpha * x1_ref[...] + gate2 * y.reshape(bt, tl, d)
    mu = jnp.mean(z, axis=-1, keepdims=True)
    zc = z - mu
    var = jnp.mean(zc * zc, axis=-1, keepdims=True)
    o_ref[...] = zc * lax.rsqrt(var + LN_EPS) * lng_ref[0, 1:2, :] + lnb_ref[0, 1:2, :]


def _combine(layer, dest, yb, x1, rf, mod, ln_g, ln_b, *, bt, tl, alpha):
    nb, seq, d = x1.shape
    r = bt * tl
    n_t = seq // tl
    return pl.pallas_call(
        functools.partial(_combine_kernel, alpha=alpha),
        grid=(nb // bt, n_t),
        in_specs=[
            pl.BlockSpec((2 * r,), lambda b, t: (b * n_t + t,), memory_space=pltpu.SMEM),
            pl.BlockSpec(memory_space=pl.ANY),
            pl.BlockSpec((bt, tl, d), lambda b, t: (b, t, 0)),
            pl.BlockSpec((r, LANES), lambda b, t: (b * n_t + t, 0)),
            pl.BlockSpec((bt, 1, 6 * d), lambda b, t: (b, 0, 0)),
            pl.BlockSpec((1, 2, d), lambda b, t: (layer, 0, 0)),
            pl.BlockSpec((1, 2, d), lambda b, t: (layer, 0, 0)),
        ],
        out_specs=pl.BlockSpec((bt, tl, d), lambda b, t: (b, t, 0)),
        out_shape=jax.ShapeDtypeStruct((nb, seq, d), F32),
        scratch_shapes=[
            pltpu.VMEM((r, d), F32),
            pltpu.VMEM((r, d), F32),
            pltpu.SemaphoreType.DMA(()),
        ],
        compiler_params=pltpu.CompilerParams(
            dimension_semantics=("arbitrary", "arbitrary"), vmem_limit_bytes=VMEM_LIMIT),
        name="moe_combine",
    )(dest, yb, x1, rf, mod, ln_g, ln_b)


def _moe(layer, u2, ri, rf, counts, x1, mod, ln_g, ln_b, wg16, wu16, wd16, *, bt, tl, alpha):
    n_tok, d = u2.shape
    n_slot = 2 * n_tok
    n_blk = -(-(n_slot + N_EXPERTS * (MOE_BLOCK - 1)) // MOE_BLOCK)
    n_rows = n_blk * MOE_BLOCK
    cnt = counts[0, :N_EXPERTS]
    padded = (cnt + MOE_BLOCK - 1) // MOE_BLOCK * MOE_BLOCK
    pad_end = jnp.cumsum(padded).astype(I32)
    pad_start = pad_end - padded
    eid = ri[:, 0:2].reshape(n_slot)
    rank = ri[:, 2:4].reshape(n_slot)
    dest = (pad_start[eid] + rank).astype(I32)
    blk_start = jnp.arange(n_blk, dtype=I32) * MOE_BLOCK
    blk_e = jnp.minimum(jnp.sum(pad_end[None, :] <= blk_start[:, None], axis=1), N_EXPERTS - 1).astype(I32)
    n_used = (pad_end[N_EXPERTS - 1:] // MOE_BLOCK).astype(I32)
    xs = _scatter(u2, dest, pad_end, n_rows, tile=bt * tl)
    yb = _experts(layer, xs, blk_e, n_used, wg16, wu16, wd16)
    return _combine(layer, dest, yb, x1, rf, mod, ln_g, ln_b, bt=bt, tl=tl, alpha=alpha)


def kernel(x_prompt, x_sample, state_conv, state_hgrn, c_prompt, c_sample, hgrn_lb_logits,
           w_mod, b_mod, w_in, conv_w, gnorm_w, w_out, ln_g, ln_b,
           w_router_group, w_router_expert, w_e_gate, w_e_up, w_e_down):
    depth, d, _ = w_in.shape
    nb_p, seq_p, _ = x_prompt.shape
    nb_s, seq_s, _ = x_sample.shape
    alpha = float((2 * depth) ** 0.25)

    mod_all = _modulation(jnp.concatenate([c_prompt, c_sample], axis=0), w_mod, b_mod)
    mod_p = mod_all[:, :nb_p].reshape(depth, nb_p, 1, 6 * d)
    mod_s = mod_all[:, nb_p:].reshape(depth, nb_s, 1, 6 * d)

    w_in16 = w_in.astype(BF16)
    w_out16 = w_out.astype(BF16)
    wg16 = w_e_gate.astype(BF16)
    wu16 = w_e_up.astype(BF16)
    wd16 = w_e_down.astype(BF16)
    gnorm3 = gnorm_w.reshape(depth, 1, HEAD_DIM)
    w_r = jnp.concatenate(
        [w_router_expert, w_router_group,
         jnp.zeros((depth, d, LANES - N_EXPERTS - N_GROUPS), F32)], axis=-1)

    tl_p = min(512, seq_p)
    bt_s = 16
    yp, ys = x_prompt, x_sample
    conv_p, hgrn_p, conv_s, hgrn_s = [], [], [], []
    for l in range(depth):
        x1, u2, ri, rf, cb, sb, cnt = _mixer(
            l, yp, mod_p[l], hgrn_lb_logits, w_in16, conv_w, gnorm3, w_out16, ln_g, ln_b, w_r,
            None, None, bt=1, tl=tl_p, chunk=64)
        yp = _moe(l, u2, ri, rf, cnt, x1, mod_p[l], ln_g, ln_b, wg16, wu16, wd16,
                  bt=1, tl=tl_p, alpha=alpha)
        conv_p.append(cb)
        hgrn_p.append(sb)
        x1, u2, ri, rf, cs, ss, cnt = _mixer(
            l, ys, mod_s[l], hgrn_lb_logits, w_in16, conv_w, gnorm3, w_out16, ln_g, ln_b, w_r,
            state_conv, state_hgrn, bt=bt_s, tl=seq_s, chunk=seq_s)
        ys = _moe(l, u2, ri, rf, cnt, x1, mod_s[l], ln_g, ln_b, wg16, wu16, wd16,
                  bt=64, tl=seq_s, alpha=alpha)
        conv_s.append(cs)
        hgrn_s.append(ss)
    return (yp, ys, jnp.stack(conv_p), jnp.stack(hgrn_p), jnp.stack(conv_s), jnp.stack(hgrn_s))
```

```python
import functools

import jax
import jax.numpy as jnp
from jax import lax
from jax.experimental import pallas as pl
from jax.experimental.pallas import tpu as pltpu

F32 = jnp.float32
BF16 = jnp.bfloat16
I32 = jnp.int32

CONV_W = 3
HEAD_DIM = 128
N_GROUPS = 4
EXPERTS_PER_GROUP = 8
N_EXPERTS = N_GROUPS * EXPERTS_PER_GROUP
LN_EPS = 1e-5
RMS_EPS = 1e-6
LANES = 128
SUBLANES = 8
MOE_BLOCK = 256
TILE_META = 2 * LANES
VMEM_LIMIT = 56 * 1024 * 1024
NEG = -1e30
RUN_PAD = N_EXPERTS * (SUBLANES - 1)
LOG2E = 1.4426950408889634


def _seg_row(x, seg, row):
    r, n = x.shape
    x3 = x.reshape(r // seg, seg, n)
    return jnp.broadcast_to(x3[:, row:row + 1, :], x3.shape).reshape(r, n)


def _roll8(x, shift):
    r, n = x.shape
    return pltpu.roll(x.reshape(r // SUBLANES, SUBLANES, n), shift, 1).reshape(r, n)


def _dot_nt(a, b):
    return lax.dot_general(a, b, (((1,), (1,)), ((), ())), preferred_element_type=F32)


def _dot_tn(a, b):
    return lax.dot_general(a, b, (((0,), (0,)), ((), ())), preferred_element_type=F32)


def _split16(x):
    hi = x.astype(BF16)
    return hi, (x - hi.astype(F32)).astype(BF16)


def _layer_norm(z, g, b):
    mu = jnp.mean(z, axis=-1, keepdims=True)
    zc = z - mu
    var = jnp.mean(zc * zc, axis=-1, keepdims=True)
    return zc * lax.rsqrt(var + LN_EPS) * g + b


def _copy_run(src_ref, dst_ref, src0, dst0, n, sem, max_rows, wait):
    off = jnp.int32(0)
    p = max_rows
    while p >= SUBLANES:
        @pl.when((n & p) != 0)
        def _(off=off, p=p):
            cp = pltpu.make_async_copy(
                src_ref.at[pl.ds(pl.multiple_of(src0 + off, SUBLANES), p)],
                dst_ref.at[pl.ds(pl.multiple_of(dst0 + off, SUBLANES), p)], sem)
            if wait:
                cp.wait()
            else:
                cp.start()
        off = off + (n & p)
        p //= 2


def _mod_kernel(c_ref, w_ref, b_ref, o_ref):
    c = c_ref[...]
    a = (c * (0.5 + 0.5 * jnp.tanh(0.5 * c))).astype(BF16)
    o_ref[0] = jnp.dot(a, w_ref[0].astype(BF16), preferred_element_type=F32) + b_ref[0]


def _modulation(c_all, w_mod, b_mod):
    depth, d, d6 = w_mod.shape
    nb = c_all.shape[0]
    n_col = d6 // d
    return pl.pallas_call(
        _mod_kernel,
        grid=(depth, n_col),
        in_specs=[
            pl.BlockSpec((nb, d), lambda l, j: (0, 0)),
            pl.BlockSpec((1, d, d), lambda l, j: (l, 0, j)),
            pl.BlockSpec((1, 1, d), lambda l, j: (l, 0, j)),
        ],
        out_specs=pl.BlockSpec((1, nb, d), lambda l, j: (l, 0, j)),
        out_shape=jax.ShapeDtypeStruct((depth, nb, d6), F32),
        compiler_params=pltpu.CompilerParams(
            dimension_semantics=("arbitrary", "arbitrary"), vmem_limit_bytes=VMEM_LIMIT),
        name="modulation",
    )(c_all, w_mod, b_mod.reshape(depth, 1, d6))


def _mixer_kernel(*refs, layer, bt, tl, chunk, ls, has_state, alpha):
    if has_state:
        (x_ref, mod_ref, lbl_ref, w_in_ref, cw_ref, gn_ref, w_out_ref, lng_ref, lnb_ref, wr_ref,
         conv0_ref, s0_ref,
         x1_ref, xsl_ref, ri_ref, rf_ref, meta_ref, conv_out_ref, s_out_ref, cnt_ref,
         ymix_ref, s_run_ref, tail_ref, cnt_run_ref) = refs
    else:
        (x_ref, mod_ref, lbl_ref, w_in_ref, cw_ref, gn_ref, w_out_ref, lng_ref, lnb_ref, wr_ref,
         x1_ref, xsl_ref, ri_ref, rf_ref, meta_ref, conv_out_ref, s_out_ref, cnt_ref,
         ymix_ref, s_run_ref, tail_ref, cnt_run_ref) = refs
    d = x_ref.shape[-1]
    conv_dim = d // 2
    hg = d - conv_dim
    heads = hg // HEAD_DIM
    r = bt * tl
    n_units = r // chunk
    first_tile = pl.program_id(1) == 0

    @pl.when(jnp.logical_and(pl.program_id(0) == 0, first_tile))
    def _():
        cnt_run_ref[...] = jnp.zeros_like(cnt_run_ref)

    if not has_state:
        @pl.when(first_tile)
        def _():
            s_run_ref[...] = jnp.zeros_like(s_run_ref)
            tail_ref[...] = jnp.zeros_like(tail_ref)

    lbl = lbl_ref[...]
    lbe = jnp.exp(lbl - jnp.max(lbl, axis=0, keepdims=True))
    lbp = lbe / jnp.sum(lbe, axis=0, keepdims=True)
    lb_all = lbp[0:1] * 0.0
    for j in range(1, layer + 1):
        lb_all = lb_all + lbp[j:j + 1]

    x3 = x_ref[...]
    mod3 = mod_ref[...]
    shift1, scale1, gate1 = (mod3[:, :, k * d:(k + 1) * d] for k in range(3))
    shift2, scale2 = (mod3[:, :, k * d:(k + 1) * d] for k in range(3, 5))
    u = (x3 * (1.0 + scale1) + shift1).reshape(r, d).astype(BF16)

    def proj(k, width):
        return jnp.dot(u, w_in_ref[0, :, k:k + width], preferred_element_type=F32)

    cb = proj(0, conv_dim)
    cc = proj(conv_dim, conv_dim)
    cv = proj(2 * conv_dim, conv_dim)
    v = cc * cv
    prev = conv0_ref[0] if has_state else tail_ref[...]
    rows_c = lax.broadcasted_iota(I32, (r, conv_dim), 0)
    pos_t = rows_c & (tl - 1)
    prev1 = jnp.broadcast_to(prev[:, 1:2, :], (bt, tl, conv_dim)).reshape(r, conv_dim)
    prev0 = jnp.broadcast_to(prev[:, 0:1, :], (bt, tl, conv_dim)).reshape(r, conv_dim)
    v1 = jnp.where(pos_t >= 1, pltpu.roll(v, 1, 0), prev1)
    v2 = jnp.where(pos_t >= 2, pltpu.roll(v, 2, 0), jnp.where(pos_t == 1, prev1, prev0))
    cw = cw_ref[0]
    y_conv = cb * (cw[0:1] * v2 + cw[1:2] * v1 + cw[2:3] * v)
    ymix_ref[:, 0:conv_dim] = y_conv.astype(BF16)
    new_tail = v.reshape(bt, tl, conv_dim)[:, tl - 2:tl, :]
    conv_out_ref[...] = new_tail
    if not has_state:
        tail_ref[...] = new_tail

    rows = lax.broadcasted_iota(I32, (r, HEAD_DIM), 0)
    pos8 = rows & (SUBLANES - 1)
    pos_c = rows & (chunk - 1)
    gn = gn_ref[0]
    t_i = lax.broadcasted_iota(I32, (chunk, chunk), 0)
    s_i = lax.broadcasted_iota(I32, (chunk, chunk), 1)
    q_off = 3 * conv_dim
    q_all = proj(q_off, hg)
    f_all = proj(q_off + hg, hg)
    i_all = proj(q_off + 2 * hg, hg)
    g_all = proj(q_off + 3 * hg, hg)
    for h in range(heads):
        lo = h * HEAD_DIM
        hs = slice(lo, lo + HEAD_DIM)
        lb = lb_all[:, hs]
        qz, fz, vv, gz = q_all[:, hs], f_all[:, hs], i_all[:, hs], g_all[:, hs]
        qq = qz * (0.5 + 0.5 * jnp.tanh(0.5 * qz)) * (HEAD_DIM ** -0.5)
        tf = jnp.tanh(0.5 * fz)
        lf = jnp.log2(lb + (1.0 - lb) * (0.5 + 0.5 * tf))
        kk = (1.0 - lb) * (0.5 - 0.5 * tf)
        b = lf
        step = 1
        while step < chunk:
            b = b + jnp.where(pos_c >= step, pltpu.roll(b, step, 0), 0.0)
            step *= 2
        o = jnp.sum(qq * kk, axis=-1, keepdims=True) * vv
        for dd in range(1, SUBLANES):
            e = qq * _roll8(kk, dd) * jnp.exp2(b - _roll8(b, dd))
            a = jnp.sum(e, axis=-1, keepdims=True)
            o = o + jnp.where(pos8 >= dd, a * _roll8(vv, dd), 0.0)
        vv16 = vv.astype(BF16)
        o_units = [None] * n_units
        if chunk > SUBLANES:
            a_units = [jnp.zeros((chunk, chunk), F32) for _ in range(n_units)]
            m = SUBLANES
            while m < chunk:
                sh = m.bit_length()
                br = _seg_row(b, 2 * m, m - 1)
                upper = (rows & (2 * m - 1)) >= m
                y = jnp.exp2(jnp.minimum(jnp.where(upper, b - br, br - b), 0.0))
                qm = (qq * y).astype(BF16)
                km = (kk * y).astype(BF16)
                same = (t_i >> sh) == (s_i >> sh)
                mask = jnp.where(same, jnp.where((t_i & (2 * m - 1)) >= m,
                                                 jnp.where((s_i & (2 * m - 1)) < m, 1.0, 0.0), 0.0), 0.0)
                for c in range(n_units):
                    sl = slice(c * chunk, (c + 1) * chunk)
                    a_units[c] = a_units[c] + mask * _dot_nt(qm[sl], km[sl])
                m *= 2
            for c in range(n_units):
                sl = slice(c * chunk, (c + 1) * chunk)
                o_units[c] = jnp.dot(a_units[c].astype(BF16), vv16[sl], preferred_element_type=F32)
        bend = _seg_row(b, chunk, chunk - 1)
        qt = (qq * jnp.exp2(b)).astype(BF16)
        kh = (kk * jnp.exp2(bend - b)).astype(BF16)
        bl = b.reshape(n_units, chunk, HEAD_DIM)[:, chunk - 1, :]
        bl = jnp.concatenate([bl, jnp.zeros((HEAD_DIM - n_units, HEAD_DIM), F32)], axis=0)
        dec_t = jnp.exp2(bl.T)
        if not has_state:
            s_cur = s_run_ref[h]
        for c in range(n_units):
            sl = slice(c * chunk, (c + 1) * chunk)
            if has_state:
                s_cur = s0_ref[0, c, h]
            o_s = jnp.dot(qt[sl], s_cur.astype(BF16), preferred_element_type=F32)
            o_units[c] = o_s if o_units[c] is None else o_units[c] + o_s
            s_new = dec_t[:, c:c + 1] * s_cur + _dot_tn(kh[sl], vv16[sl])
            if has_state:
                s_out_ref[c, h] = s_new
            else:
                s_cur = s_new
        if not has_state:
            s_run_ref[h] = s_cur
            s_out_ref[0, h] = s_cur
        o = o + jnp.concatenate(o_units, axis=0)
        o = o * lax.rsqrt(jnp.mean(o * o, axis=-1, keepdims=True) + RMS_EPS)
        o = o * gn * (gz * (0.5 + 0.5 * jnp.tanh(0.5 * gz)))
        ymix_ref[:, conv_dim + lo:conv_dim + lo + HEAD_DIM] = o.astype(BF16)

    mix = jnp.dot(ymix_ref[...], w_out_ref[0], preferred_element_type=F32).reshape(bt, tl, d)
    x1 = _layer_norm(alpha * x3 + gate1 * mix, lng_ref[0, 0:1, :], lnb_ref[0, 0:1, :])
    x1_ref[...] = x1
    u2 = (x1 * (1.0 + scale2) + shift2).reshape(r, d)

    u_hi, u_lo = _split16(u2)
    w_hi, w_lo = _split16(wr_ref[0])
    logits = (jnp.dot(u_hi, w_hi, preferred_element_type=F32)
              + jnp.dot(u_lo, w_hi, preferred_element_type=F32)
              + jnp.dot(u_hi, w_lo, preferred_element_type=F32))
    lane = lax.broadcasted_iota(I32, (r, LANES), 1)
    is_g = jnp.logical_and(lane >= N_EXPERTS, lane < N_EXPERTS + N_GROUPS)
    gl = jnp.where(is_g, logits, NEG)
    gm = jnp.max(gl, axis=-1, keepdims=True)
    grp = jnp.min(jnp.where(gl == gm, lane - N_EXPERTS, LANES), axis=-1, keepdims=True)
    p_sel = 1.0 / jnp.sum(jnp.where(is_g, jnp.exp(gl - gm), 0.0), axis=-1, keepdims=True)
    in_grp = jnp.logical_and(lane < N_EXPERTS, (lane >> 3) == grp)
    el = jnp.where(in_grp, logits, NEG)
    m1 = jnp.max(el, axis=-1, keepdims=True)
    i1 = jnp.min(jnp.where(jnp.logical_and(in_grp, el == m1), lane, LANES), axis=-1, keepdims=True)
    in2 = jnp.logical_and(in_grp, lane != i1)
    el2 = jnp.where(in2, logits, NEG)
    m2 = jnp.max(el2, axis=-1, keepdims=True)
    i2 = jnp.min(jnp.where(jnp.logical_and(in2, el2 == m2), lane, LANES), axis=-1, keepdims=True)
    e21 = jnp.exp(m2 - m1)
    w1 = p_sel / (1.0 + e21)
    w2 = p_sel * e21 / (1.0 + e21)

    oh1 = lane == i1
    oh2 = lane == i2
    ohs = jnp.where(oh1, 1.0, 0.0) + jnp.where(oh2, 1.0, 0.0)
    tr = lax.broadcasted_iota(I32, (r, r), 0)
    tc = lax.broadcasted_iota(I32, (r, r), 1)
    ltri = jnp.where(tr > tc, 1.0, 0.0).astype(BF16)
    before = jnp.dot(ltri, ohs.astype(BF16), preferred_element_type=F32)
    n_tile = jnp.sum(ohs, axis=0, keepdims=True)
    n_tile = ((n_tile.astype(I32) + (SUBLANES - 1)) & -SUBLANES).astype(F32)
    ur = lax.broadcasted_iota(I32, (LANES, LANES), 0)
    uc = lax.broadcasted_iota(I32, (LANES, LANES), 1)
    upper_tri = jnp.where(ur < uc, 1.0, 0.0).astype(BF16)
    n8 = jnp.broadcast_to(n_tile, (SUBLANES, LANES))
    n_hi, n_lo = _split16(n8)
    off_row = (jnp.dot(n_hi, upper_tri, preferred_element_type=F32)
               + jnp.dot(n_lo, upper_tri, preferred_element_type=F32))[0:1]
    place = before + off_row
    lpos1 = jnp.sum(jnp.where(oh1, place, 0.0), axis=-1, keepdims=True).astype(I32)
    lpos2 = jnp.sum(jnp.where(oh2, place, 0.0), axis=-1, keepdims=True).astype(I32)
    slot = lax.broadcasted_iota(I32, (r, ls), 1)
    perm_t = jnp.where(jnp.logical_or(slot == lpos1, slot == lpos2), 1.0, 0.0).astype(BF16)
    xsl_ref[...] = _dot_tn(perm_t, u_hi)

    cnt = cnt_run_ref[...] + n_tile
    cnt_run_ref[...] = cnt
    cnt_ref[...] = cnt.astype(I32)
    meta_ref[0, 0:1, :] = n_tile.astype(I32)
    meta_ref[0, 1:2, :] = off_row.astype(I32)
    ri_ref[...] = jnp.where(lane == 0, lpos1, jnp.where(lane == 1, lpos2, 0))
    rf_ref[...] = jnp.where(lane == 0, w1, jnp.where(lane == 1, w2, 0.0))


def _mixer(layer, x, mod, lb_logits, w_in16, conv_w, gnorm3, w_out16, ln_g, ln_b, w_r,
           state_conv, state_hgrn, *, bt, tl, chunk, ls):
    nb, seq, d = x.shape
    depth = w_in16.shape[0]
    conv_dim = d // 2
    hg = d - conv_dim
    heads = hg // HEAD_DIM
    r = bt * tl
    has_state = state_conv is not None
    if has_state:
        assert seq == tl == chunk
    n_t = seq // tl
    n_tiles = (nb // bt) * n_t
    const = dict(pipeline_mode=pl.Buffered(1))
    in_specs = [
        pl.BlockSpec((bt, tl, d), lambda b, t: (b, t, 0)),
        pl.BlockSpec((bt, 1, 6 * d), lambda b, t: (b, 0, 0)),
        pl.BlockSpec(lb_logits.shape, lambda b, t: (0, 0), **const),
        pl.BlockSpec((1, d, w_in16.shape[2]), lambda b, t: (layer, 0, 0), **const),
        pl.BlockSpec((1, CONV_W, conv_dim), lambda b, t: (layer, 0, 0), **const),
        pl.BlockSpec((1, 1, HEAD_DIM), lambda b, t: (layer, 0, 0), **const),
        pl.BlockSpec((1, d, d), lambda b, t: (layer, 0, 0), **const),
        pl.BlockSpec((1, 2, d), lambda b, t: (layer, 0, 0), **const),
        pl.BlockSpec((1, 2, d), lambda b, t: (layer, 0, 0), **const),
        pl.BlockSpec((1, d, LANES), lambda b, t: (layer, 0, 0), **const),
    ]
    args = [x, mod, lb_logits, w_in16, conv_w, gnorm3, w_out16, ln_g, ln_b, w_r]
    if has_state:
        in_specs += [
            pl.BlockSpec((1, bt, CONV_W - 1, conv_dim), lambda b, t: (layer, b, 0, 0)),
            pl.BlockSpec((1, bt, heads, HEAD_DIM, HEAD_DIM), lambda b, t: (layer, b, 0, 0, 0)),
        ]
        args += [state_conv, state_hgrn]
    out_shape = (
        jax.ShapeDtypeStruct((nb, seq, d), F32),
        jax.ShapeDtypeStruct((n_tiles * ls, d), F32),
        jax.ShapeDtypeStruct((nb * seq, LANES), I32),
        jax.ShapeDtypeStruct((nb * seq, LANES), F32),
        jax.ShapeDtypeStruct((n_tiles, 2, LANES), I32),
        jax.ShapeDtypeStruct((nb, CONV_W - 1, conv_dim), F32),
        jax.ShapeDtypeStruct((nb, heads, HEAD_DIM, HEAD_DIM), F32),
        jax.ShapeDtypeStruct((1, LANES), I32),
    )
    out_specs = (
        pl.BlockSpec((bt, tl, d), lambda b, t: (b, t, 0)),
        pl.BlockSpec((ls, d), lambda b, t: (b * n_t + t, 0)),
        pl.BlockSpec((r, LANES), lambda b, t: (b * n_t + t, 0)),
        pl.BlockSpec((r, LANES), lambda b, t: (b * n_t + t, 0)),
        pl.BlockSpec((1, 2, LANES), lambda b, t: (b * n_t + t, 0, 0)),
        pl.BlockSpec((bt, CONV_W - 1, conv_dim), lambda b, t: (b, 0, 0)),
        pl.BlockSpec((bt, heads, HEAD_DIM, HEAD_DIM), lambda b, t: (b, 0, 0, 0)),
        pl.BlockSpec((1, LANES), lambda b, t: (0, 0)),
    )
    scratch = [
        pltpu.VMEM((r, d), BF16),
        pltpu.VMEM((heads, HEAD_DIM, HEAD_DIM), F32),
        pltpu.VMEM((1, CONV_W - 1, conv_dim), F32),
        pltpu.VMEM((1, LANES), F32),
    ]
    kern = functools.partial(_mixer_kernel, layer=layer, bt=bt, tl=tl, chunk=chunk, ls=ls,
                             has_state=has_state, alpha=float((2 * depth) ** 0.25))
    return pl.pallas_call(
        kern, grid=(nb // bt, n_t), in_specs=in_specs, out_specs=out_specs, out_shape=out_shape,
        scratch_shapes=scratch,
        compiler_params=pltpu.CompilerParams(
            dimension_semantics=("arbitrary", "arbitrary"), vmem_limit_bytes=VMEM_LIMIT),
        name="mixer_state" if has_state else "mixer_seq",
    )(*args)


def _expert_kernel(blk_e_ref, blk_s_ref, n_used_ref, cnt_ref, meta_ref,
                   xsl_ref, wg_ref, wu_ref, wd_ref, yb_ref,
                   xbuf_ref, wg16_ref, wu16_ref, wd16_ref, sem, *, n_tiles, tile_slots):
    j = pl.program_id(0)
    n_used = n_used_ref[0]

    def walk(jj, slot, wait):
        e = blk_e_ref[jj]
        s = blk_s_ref[jj]
        n_valid = jnp.minimum(MOE_BLOCK, cnt_ref[e] - s)

        def body(i, c):
            n = meta_ref[i * TILE_META + e]
            off = meta_ref[i * TILE_META + LANES + e]
            lo = jnp.maximum(c, s)
            hi = jnp.minimum(c + n, s + n_valid)

            @pl.when(hi > lo)
            def _():
                _copy_run(xsl_ref, xbuf_ref.at[slot], i * tile_slots + off + (lo - c), lo - s,
                          hi - lo, sem.at[slot], MOE_BLOCK, wait)
            return c + n

        lax.fori_loop(0, n_tiles, body, jnp.int32(0))

    @pl.when(j == 0)
    def _():
        xbuf_ref[...] = jnp.zeros_like(xbuf_ref)
        walk(0, 0, False)

    @pl.when(j + 1 < n_used)
    def _():
        walk(j + 1, (j + 1) & 1, False)

    @pl.when(j < n_used)
    def _():
        walk(j, j & 1, True)
        changed = jnp.logical_or(j == 0, blk_e_ref[j] != blk_e_ref[jnp.maximum(j - 1, 0)])

        @pl.when(changed)
        def _():
            wg16_ref[...] = wg_ref[0, 0].astype(BF16)
            wu16_ref[...] = wu_ref[0, 0].astype(BF16)
            wd16_ref[...] = wd_ref[0, 0].astype(BF16)

        x = xbuf_ref[j & 1].astype(BF16)
        g = jnp.dot(x, wg16_ref[...], preferred_element_type=F32)
        up = jnp.dot(x, wu16_ref[...], preferred_element_type=F32)
        hid = (g * (0.5 + 0.5 * jnp.tanh(0.5 * g)) * up).astype(BF16)
        yb_ref[...] = jnp.dot(hid, wd16_ref[...], preferred_element_type=F32)


def _experts(layer, xsl, blk_e, blk_s, n_used, cnt, meta, w_gate, w_up, w_down, *, n_tiles, tile_slots):
    n_slot, d = xsl.shape
    de = w_gate.shape[-1]
    n_blk = blk_e.shape[0]

    def blk(j, be, bs, nu, ct, mt):
        return jnp.minimum(j, nu[0] - 1)

    def w_idx(j, be, bs, nu, ct, mt):
        return (layer, be[blk(j, be, bs, nu, ct, mt)], 0, 0)

    grid_spec = pltpu.PrefetchScalarGridSpec(
        num_scalar_prefetch=5,
        grid=(n_blk,),
        in_specs=[
            pl.BlockSpec(memory_space=pl.ANY),
            pl.BlockSpec((1, 1, d, de), w_idx),
            pl.BlockSpec((1, 1, d, de), w_idx),
            pl.BlockSpec((1, 1, de, d), w_idx),
        ],
        out_specs=pl.BlockSpec((MOE_BLOCK, d), lambda j, *a: (blk(j, *a), 0)),
        scratch_shapes=[
            pltpu.VMEM((2, MOE_BLOCK, d), F32),
            pltpu.VMEM((d, de), BF16),
            pltpu.VMEM((d, de), BF16),
            pltpu.VMEM((de, d), BF16),
            pltpu.SemaphoreType.DMA((2,)),
        ],
    )
    return pl.pallas_call(
        functools.partial(_expert_kernel, n_tiles=n_tiles, tile_slots=tile_slots),
        grid_spec=grid_spec,
        out_shape=jax.ShapeDtypeStruct((n_blk * MOE_BLOCK, d), F32),
        compiler_params=pltpu.CompilerParams(
            dimension_semantics=("arbitrary",), vmem_limit_bytes=VMEM_LIMIT),
        name="moe_experts",
    )(blk_e, blk_s, n_used, cnt, meta, xsl, w_gate, w_up, w_down)


def _combine_kernel(cnt_ref, meta_ref, yb_ref, x1_ref, ri_ref, rf_ref, mod_ref, lng_ref, lnb_ref,
                    o_ref, ybuf_ref, start_ref, cur_ref, sem, *, alpha, n_t):
    bt, tl, d = x1_ref.shape
    ls = ybuf_ref.shape[1]
    r = bt * tl
    step = pl.program_id(0) * n_t + pl.program_id(1)
    n_steps = pl.num_programs(0) * n_t

    def walk(i, slot, wait):
        def body(e, carry):
            n = meta_ref[i * TILE_META + e]
            off = meta_ref[i * TILE_META + LANES + e]
            src = 0 if wait else start_ref[e] + cur_ref[e]
            _copy_run(yb_ref, ybuf_ref.at[slot], src, off, n, sem.at[slot], r, wait)
            if not wait:
                cur_ref[e] = cur_ref[e] + n
            return carry

        lax.fori_loop(0, N_EXPERTS, body, 0)

    @pl.when(step == 0)
    def _():
        def init(e, acc):
            start_ref[e] = acc
            cur_ref[e] = 0
            return acc + ((cnt_ref[e] + MOE_BLOCK - 1) // MOE_BLOCK) * MOE_BLOCK

        lax.fori_loop(0, N_EXPERTS, init, jnp.int32(0))
        ybuf_ref[...] = jnp.zeros_like(ybuf_ref)
        walk(0, 0, False)

    @pl.when(step + 1 < n_steps)
    def _():
        walk(step + 1, (step + 1) & 1, False)

    walk(step, step & 1, True)
    ys = ybuf_ref[step & 1].astype(BF16)
    ri = ri_ref[...]
    rf = rf_ref[...]
    slot = lax.broadcasted_iota(I32, (r, ls), 1)
    sel1 = jnp.where(slot == ri[:, 0:1], 1.0, 0.0).astype(BF16)
    sel2 = jnp.where(slot == ri[:, 1:2], 1.0, 0.0).astype(BF16)
    y = (rf[:, 0:1] * jnp.dot(sel1, ys, preferred_element_type=F32)
         + rf[:, 1:2] * jnp.dot(sel2, ys, preferred_element_type=F32))
    gate2 = mod_ref[:, :, 5 * d:6 * d]
    z = alpha * x1_ref[...] + gate2 * y.reshape(bt, tl, d)
    o_ref[...] = _layer_norm(z, lng_ref[0, 1:2, :], lnb_ref[0, 1:2, :])


def _combine(layer, cnt, meta, yb, x1, ri, rf, mod, ln_g, ln_b, *, bt, tl, ls, alpha):
    nb, seq, d = x1.shape
    r = bt * tl
    n_t = seq // tl
    grid_spec = pltpu.PrefetchScalarGridSpec(
        num_scalar_prefetch=2,
        grid=(nb // bt, n_t),
        in_specs=[
            pl.BlockSpec(memory_space=pl.ANY),
            pl.BlockSpec((bt, tl, d), lambda b, t, *a: (b, t, 0)),
            pl.BlockSpec((r, LANES), lambda b, t, *a: (b * n_t + t, 0)),
            pl.BlockSpec((r, LANES), lambda b, t, *a: (b * n_t + t, 0)),
            pl.BlockSpec((bt, 1, 6 * d), lambda b, t, *a: (b, 0, 0)),
            pl.BlockSpec((1, 2, d), lambda b, t, *a: (layer, 0, 0)),
            pl.BlockSpec((1, 2, d), lambda b, t, *a: (layer, 0, 0)),
        ],
        out_specs=pl.BlockSpec((bt, tl, d), lambda b, t, *a: (b, t, 0)),
        scratch_shapes=[
            pltpu.VMEM((2, ls, d), F32),
            pltpu.SMEM((N_EXPERTS,), I32),
            pltpu.SMEM((N_EXPERTS,), I32),
            pltpu.SemaphoreType.DMA((2,)),
        ],
    )
    return pl.pallas_call(
        functools.partial(_combine_kernel, alpha=alpha, n_t=n_t),
        grid_spec=grid_spec,
        out_shape=jax.ShapeDtypeStruct((nb, seq, d), F32),
        compiler_params=pltpu.CompilerParams(
            dimension_semantics=("arbitrary", "arbitrary"), vmem_limit_bytes=VMEM_LIMIT),
        name="moe_combine",
    )(cnt, meta, yb, x1, ri, rf, mod, ln_g, ln_b)


def _moe(layer, xsl, ri, rf, meta, counts, x1, mod, ln_g, ln_b, w_gate, w_up, w_down, *, bt, tl, ls, alpha):
    n_tiles = xsl.shape[0] // ls
    n_slot = 2 * n_tiles * bt * tl
    n_blk = -(-(n_slot + n_tiles * RUN_PAD + N_EXPERTS * (MOE_BLOCK - 1)) // MOE_BLOCK)
    cnt = counts[0]
    padded = (cnt[:N_EXPERTS] + MOE_BLOCK - 1) // MOE_BLOCK * MOE_BLOCK
    pad_end = jnp.cumsum(padded).astype(I32)
    blk_start = jnp.arange(n_blk, dtype=I32) * MOE_BLOCK
    blk_e = jnp.minimum(jnp.sum(pad_end[None, :] <= blk_start[:, None], axis=1), N_EXPERTS - 1).astype(I32)
    blk_s = blk_start - (pad_end - padded)[blk_e]
    n_used = pad_end[N_EXPERTS - 1:] // MOE_BLOCK
    meta_flat = meta.reshape(n_tiles * TILE_META)
    yb = _experts(layer, xsl, blk_e, blk_s, n_used, cnt, meta_flat, w_gate, w_up, w_down,
                  n_tiles=n_tiles, tile_slots=ls)
    return _combine(layer, cnt, meta_flat, yb, x1, ri, rf, mod, ln_g, ln_b, bt=bt, tl=tl, ls=ls, alpha=alpha)


def _slot_capacity(tokens):
    return -(-(2 * tokens + RUN_PAD) // MOE_BLOCK) * MOE_BLOCK


def kernel(x_prompt, x_sample, state_conv, state_hgrn, c_prompt, c_sample, hgrn_lb_logits,
           w_mod, b_mod, w_in, conv_w, gnorm_w, w_out, ln_g, ln_b,
           w_router_group, w_router_expert, w_e_gate, w_e_up, w_e_down):
    depth, d, _ = w_in.shape
    nb_p, seq_p, _ = x_prompt.shape
    nb_s, seq_s, _ = x_sample.shape
    alpha = float((2 * depth) ** 0.25)

    mod_all = _modulation(jnp.concatenate([c_prompt, c_sample], axis=0), w_mod, b_mod)
    mod_p = mod_all[:, :nb_p].reshape(depth, nb_p, 1, 6 * d)
    mod_s = mod_all[:, nb_p:].reshape(depth, nb_s, 1, 6 * d)

    w_in16 = w_in.astype(BF16)
    w_out16 = w_out.astype(BF16)
    gnorm3 = gnorm_w.reshape(depth, 1, HEAD_DIM)
    w_r = jnp.concatenate(
        [w_router_expert, w_router_group,
         jnp.zeros((depth, d, LANES - N_EXPERTS - N_GROUPS), F32)], axis=-1)

    tl_p = min(512, seq_p)
    bt_s = 16
    ls_p = _slot_capacity(tl_p)
    ls_s = _slot_capacity(bt_s * seq_s)
    yp, ys = x_prompt, x_sample
    conv_p, hgrn_p, conv_s, hgrn_s = [], [], [], []
    for l in range(depth):
        x1, xsl, ri, rf, meta, cb, sb, cnt = _mixer(
            l, yp, mod_p[l], hgrn_lb_logits, w_in16, conv_w, gnorm3, w_out16, ln_g, ln_b, w_r,
            None, None, bt=1, tl=tl_p, chunk=64, ls=ls_p)
        yp = _moe(l, xsl, ri, rf, meta, cnt, x1, mod_p[l], ln_g, ln_b, w_e_gate, w_e_up, w_e_down,
                  bt=1, tl=tl_p, ls=ls_p, alpha=alpha)
        conv_p.append(cb)
        hgrn_p.append(sb)
        x1, xsl, ri, rf, meta, cs, ss, cnt = _mixer(
            l, ys, mod_s[l], hgrn_lb_logits, w_in16, conv_w, gnorm3, w_out16, ln_g, ln_b, w_r,
            state_conv, state_hgrn, bt=bt_s, tl=seq_s, chunk=seq_s, ls=ls_s)
        ys = _moe(l, xsl, ri, rf, meta, cnt, x1, mod_s[l], ln_g, ln_b, w_e_gate, w_e_up, w_e_down,
                  bt=bt_s, tl=seq_s, ls=ls_s, alpha=alpha)
        conv_s.append(cs)
        hgrn_s.append(ss)
    return (yp, ys, jnp.stack(conv_p), jnp.stack(hgrn_p), jnp.stack(conv_s), jnp.stack(hgrn_s))
```

```python
import functools

import jax
import jax.numpy as jnp
from jax import lax
from jax.experimental import pallas as pl
from jax.experimental.pallas import tpu as pltpu

F32 = jnp.float32
BF16 = jnp.bfloat16
I32 = jnp.int32

CONV_W = 3
HEAD_DIM = 128
N_GROUPS = 4
EXPERTS_PER_GROUP = 8
N_EXPERTS = N_GROUPS * EXPERTS_PER_GROUP
LN_EPS = 1e-5
RMS_EPS = 1e-6
LANES = 128
SUBLANES = 8
MOE_BLOCK = 256
TILE_META = 2 * LANES
VMEM_LIMIT = 56 * 1024 * 1024
NEG = -1e30
RUN_PAD = N_EXPERTS * (SUBLANES - 1)
LOG2E = 1.4426950408889634


def _seg_row(x, seg, row):
    r, n = x.shape
    x3 = x.reshape(r // seg, seg, n)
    return jnp.broadcast_to(x3[:, row:row + 1, :], x3.shape).reshape(r, n)


def _roll8(x, shift):
    r, n = x.shape
    return pltpu.roll(x.reshape(r // SUBLANES, SUBLANES, n), shift, 1).reshape(r, n)


def _dot_nt(a, b):
    return lax.dot_general(a, b, (((1,), (1,)), ((), ())), preferred_element_type=F32)


def _dot_tn(a, b):
    return lax.dot_general(a, b, (((0,), (0,)), ((), ())), preferred_element_type=F32)


def _split16(x):
    hi = x.astype(BF16)
    return hi, (x - hi.astype(F32)).astype(BF16)


def _layer_norm(z, g, b):
    mu = jnp.mean(z, axis=-1, keepdims=True)
    zc = z - mu
    var = jnp.mean(zc * zc, axis=-1, keepdims=True)
    return zc * lax.rsqrt(var + LN_EPS) * g + b


def _copy_run(src_ref, dst_ref, src0, dst0, n, sem, max_rows, wait):
    off = jnp.int32(0)
    p = max_rows
    while p >= SUBLANES:
        @pl.when((n & p) != 0)
        def _(off=off, p=p):
            cp = pltpu.make_async_copy(
                src_ref.at[pl.ds(pl.multiple_of(src0 + off, SUBLANES), p)],
                dst_ref.at[pl.ds(pl.multiple_of(dst0 + off, SUBLANES), p)], sem)
            if wait:
                cp.wait()
            else:
                cp.start()
        off = off + (n & p)
        p //= 2


def _wait_rows(src_ref, dst_ref, n, sem, max_rows):
    p = max_rows
    while p >= SUBLANES:
        @pl.when((n & p) != 0)
        def _(p=p):
            pltpu.make_async_copy(src_ref.at[pl.ds(0, p)], dst_ref.at[pl.ds(0, p)], sem).wait()
        p //= 2


def _mod_kernel(c_ref, w_ref, b_ref, o_ref):
    c = c_ref[...]
    a = (c * (0.5 + 0.5 * jnp.tanh(0.5 * c))).astype(BF16)
    o_ref[0] = jnp.dot(a, w_ref[0].astype(BF16), preferred_element_type=F32) + b_ref[0]


def _modulation(c_all, w_mod, b_mod):
    depth, d, d6 = w_mod.shape
    nb = c_all.shape[0]
    n_col = d6 // d
    return pl.pallas_call(
        _mod_kernel,
        grid=(depth, n_col),
        in_specs=[
            pl.BlockSpec((nb, d), lambda l, j: (0, 0)),
            pl.BlockSpec((1, d, d), lambda l, j: (l, 0, j)),
            pl.BlockSpec((1, 1, d), lambda l, j: (l, 0, j)),
        ],
        out_specs=pl.BlockSpec((1, nb, d), lambda l, j: (l, 0, j)),
        out_shape=jax.ShapeDtypeStruct((depth, nb, d6), F32),
        compiler_params=pltpu.CompilerParams(
            dimension_semantics=("arbitrary", "arbitrary"), vmem_limit_bytes=VMEM_LIMIT),
        name="modulation",
    )(c_all, w_mod, b_mod.reshape(depth, 1, d6))


def _mixer_kernel(*refs, layer, bt, tl, chunk, ls, has_state, alpha):
    if has_state:
        (x_ref, mod_ref, lbl_ref, w_in_ref, cw_ref, gn_ref, w_out_ref, lng_ref, lnb_ref, wr_ref,
         conv0_ref, s0_ref,
         x1_ref, xsl_ref, ri_ref, rf_ref, meta_ref, conv_out_ref, s_out_ref, cnt_ref,
         ymix_ref, s_run_ref, tail_ref, cnt_run_ref) = refs
    else:
        (x_ref, mod_ref, lbl_ref, w_in_ref, cw_ref, gn_ref, w_out_ref, lng_ref, lnb_ref, wr_ref,
         x1_ref, xsl_ref, ri_ref, rf_ref, meta_ref, conv_out_ref, s_out_ref, cnt_ref,
         ymix_ref, s_run_ref, tail_ref, cnt_run_ref) = refs
    d = x_ref.shape[-1]
    conv_dim = d // 2
    hg = d - conv_dim
    heads = hg // HEAD_DIM
    r = bt * tl
    n_units = r // chunk
    first_tile = pl.program_id(1) == 0

    @pl.when(jnp.logical_and(pl.program_id(0) == 0, first_tile))
    def _():
        cnt_run_ref[...] = jnp.zeros_like(cnt_run_ref)

    if not has_state:
        @pl.when(first_tile)
        def _():
            s_run_ref[...] = jnp.zeros_like(s_run_ref)
            tail_ref[...] = jnp.zeros_like(tail_ref)

    lbl = lbl_ref[...]
    lbe = jnp.exp(lbl - jnp.max(lbl, axis=0, keepdims=True))
    lbp = lbe / jnp.sum(lbe, axis=0, keepdims=True)
    lb_all = lbp[0:1] * 0.0
    for j in range(1, layer + 1):
        lb_all = lb_all + lbp[j:j + 1]

    x3 = x_ref[...]
    mod3 = mod_ref[...]
    shift1, scale1, gate1 = (mod3[:, :, k * d:(k + 1) * d] for k in range(3))
    shift2, scale2 = (mod3[:, :, k * d:(k + 1) * d] for k in range(3, 5))
    u = (x3 * (1.0 + scale1) + shift1).reshape(r, d).astype(BF16)

    def proj(k, width):
        return jnp.dot(u, w_in_ref[0, :, k:k + width], preferred_element_type=F32)

    cb = proj(0, conv_dim)
    cc = proj(conv_dim, conv_dim)
    cv = proj(2 * conv_dim, conv_dim)
    v = cc * cv
    prev = conv0_ref[0] if has_state else tail_ref[...]
    rows_c = lax.broadcasted_iota(I32, (r, conv_dim), 0)
    pos_t = rows_c & (tl - 1)
    prev1 = jnp.broadcast_to(prev[:, 1:2, :], (bt, tl, conv_dim)).reshape(r, conv_dim)
    prev0 = jnp.broadcast_to(prev[:, 0:1, :], (bt, tl, conv_dim)).reshape(r, conv_dim)
    v1 = jnp.where(pos_t >= 1, pltpu.roll(v, 1, 0), prev1)
    v2 = jnp.where(pos_t >= 2, pltpu.roll(v, 2, 0), jnp.where(pos_t == 1, prev1, prev0))
    cw = cw_ref[0]
    y_conv = cb * (cw[0:1] * v2 + cw[1:2] * v1 + cw[2:3] * v)
    ymix_ref[:, 0:conv_dim] = y_conv.astype(BF16)
    new_tail = v.reshape(bt, tl, conv_dim)[:, tl - 2:tl, :]
    conv_out_ref[...] = new_tail
    if not has_state:
        tail_ref[...] = new_tail

    rows = lax.broadcasted_iota(I32, (r, HEAD_DIM), 0)
    pos8 = rows & (SUBLANES - 1)
    pos_c = rows & (chunk - 1)
    gn = gn_ref[0]
    t_i = lax.broadcasted_iota(I32, (chunk, chunk), 0)
    s_i = lax.broadcasted_iota(I32, (chunk, chunk), 1)
    q_off = 3 * conv_dim
    q_all = proj(q_off, hg)
    f_all = proj(q_off + hg, hg)
    i_all = proj(q_off + 2 * hg, hg)
    g_all = proj(q_off + 3 * hg, hg)
    for h in range(heads):
        lo = h * HEAD_DIM
        hs = slice(lo, lo + HEAD_DIM)
        lb = lb_all[:, hs]
        qz, fz, vv, gz = q_all[:, hs], f_all[:, hs], i_all[:, hs], g_all[:, hs]
        qq = qz * (0.5 + 0.5 * jnp.tanh(0.5 * qz)) * (HEAD_DIM ** -0.5)
        tf = jnp.tanh(0.5 * fz)
        lf = jnp.log2(lb + (1.0 - lb) * (0.5 + 0.5 * tf))
        kk = (1.0 - lb) * (0.5 - 0.5 * tf)
        b = lf
        step = 1
        while step < chunk:
            b = b + jnp.where(pos_c >= step, pltpu.roll(b, step, 0), 0.0)
            step *= 2
        o = jnp.sum(qq * kk, axis=-1, keepdims=True) * vv
        for dd in range(1, SUBLANES):
            e = qq * _roll8(kk, dd) * jnp.exp2(b - _roll8(b, dd))
            a = jnp.sum(e, axis=-1, keepdims=True)
            o = o + jnp.where(pos8 >= dd, a * _roll8(vv, dd), 0.0)
        vv16 = vv.astype(BF16)
        o_units = [None] * n_units
        if chunk > SUBLANES:
            a_units = [jnp.zeros((chunk, chunk), F32) for _ in range(n_units)]
            m = SUBLANES
            while m < chunk:
                sh = m.bit_length()
                br = _seg_row(b, 2 * m, m - 1)
                upper = (rows & (2 * m - 1)) >= m
                y = jnp.exp2(jnp.minimum(jnp.where(upper, b - br, br - b), 0.0))
                qm = (qq * y).astype(BF16)
                km = (kk * y).astype(BF16)
                same = (t_i >> sh) == (s_i >> sh)
                mask = jnp.where(same, jnp.where((t_i & (2 * m - 1)) >= m,
                                                 jnp.where((s_i & (2 * m - 1)) < m, 1.0, 0.0), 0.0), 0.0)
                for c in range(n_units):
                    sl = slice(c * chunk, (c + 1) * chunk)
                    a_units[c] = a_units[c] + mask * _dot_nt(qm[sl], km[sl])
                m *= 2
            for c in range(n_units):
                sl = slice(c * chunk, (c + 1) * chunk)
                o_units[c] = jnp.dot(a_units[c].astype(BF16), vv16[sl], preferred_element_type=F32)
        bend = _seg_row(b, chunk, chunk - 1)
        qt = (qq * jnp.exp2(b)).astype(BF16)
        kh = (kk * jnp.exp2(bend - b)).astype(BF16)
        bl = b.reshape(n_units, chunk, HEAD_DIM)[:, chunk - 1, :]
        bl = jnp.concatenate([bl, jnp.zeros((HEAD_DIM - n_units, HEAD_DIM), F32)], axis=0)
        dec_t = jnp.exp2(bl.T)
        if not has_state:
            s_cur = s_run_ref[h]
        for c in range(n_units):
            sl = slice(c * chunk, (c + 1) * chunk)
            if has_state:
                s_cur = s0_ref[0, c, h]
            o_s = jnp.dot(qt[sl], s_cur.astype(BF16), preferred_element_type=F32)
            o_units[c] = o_s if o_units[c] is None else o_units[c] + o_s
            s_new = dec_t[:, c:c + 1] * s_cur + _dot_tn(kh[sl], vv16[sl])
            if has_state:
                s_out_ref[c, h] = s_new
            else:
                s_cur = s_new
        if not has_state:
            s_run_ref[h] = s_cur
            s_out_ref[0, h] = s_cur
        o = o + jnp.concatenate(o_units, axis=0)
        o = o * lax.rsqrt(jnp.mean(o * o, axis=-1, keepdims=True) + RMS_EPS)
        o = o * gn * (gz * (0.5 + 0.5 * jnp.tanh(0.5 * gz)))
        ymix_ref[:, conv_dim + lo:conv_dim + lo + HEAD_DIM] = o.astype(BF16)

    mix = jnp.dot(ymix_ref[...], w_out_ref[0], preferred_element_type=F32).reshape(bt, tl, d)
    x1 = _layer_norm(alpha * x3 + gate1 * mix, lng_ref[0, 0:1, :], lnb_ref[0, 0:1, :])
    x1_ref[...] = x1
    u2 = (x1 * (1.0 + scale2) + shift2).reshape(r, d)

    u_hi, u_lo = _split16(u2)
    w_hi, w_lo = _split16(wr_ref[0])
    logits = (jnp.dot(u_hi, w_hi, preferred_element_type=F32)
              + jnp.dot(u_lo, w_hi, preferred_element_type=F32)
              + jnp.dot(u_hi, w_lo, preferred_element_type=F32))
    lane = lax.broadcasted_iota(I32, (r, LANES), 1)
    is_g = jnp.logical_and(lane >= N_EXPERTS, lane < N_EXPERTS + N_GROUPS)
    gl = jnp.where(is_g, logits, NEG)
    gm = jnp.max(gl, axis=-1, keepdims=True)
    grp = jnp.min(jnp.where(gl == gm, lane - N_EXPERTS, LANES), axis=-1, keepdims=True)
    p_sel = 1.0 / jnp.sum(jnp.where(is_g, jnp.exp(gl - gm), 0.0), axis=-1, keepdims=True)
    in_grp = jnp.logical_and(lane < N_EXPERTS, (lane >> 3) == grp)
    el = jnp.where(in_grp, logits, NEG)
    m1 = jnp.max(el, axis=-1, keepdims=True)
    i1 = jnp.min(jnp.where(jnp.logical_and(in_grp, el == m1), lane, LANES), axis=-1, keepdims=True)
    in2 = jnp.logical_and(in_grp, lane != i1)
    el2 = jnp.where(in2, logits, NEG)
    m2 = jnp.max(el2, axis=-1, keepdims=True)
    i2 = jnp.min(jnp.where(jnp.logical_and(in2, el2 == m2), lane, LANES), axis=-1, keepdims=True)
    e21 = jnp.exp(m2 - m1)
    w1 = p_sel / (1.0 + e21)
    w2 = p_sel * e21 / (1.0 + e21)

    oh1 = lane == i1
    oh2 = lane == i2
    ohs = jnp.where(oh1, 1.0, 0.0) + jnp.where(oh2, 1.0, 0.0)
    tr = lax.broadcasted_iota(I32, (r, r), 0)
    tc = lax.broadcasted_iota(I32, (r, r), 1)
    ltri = jnp.where(tr > tc, 1.0, 0.0).astype(BF16)
    before = jnp.dot(ltri, ohs.astype(BF16), preferred_element_type=F32)
    n_tile = jnp.sum(ohs, axis=0, keepdims=True)
    n_tile = ((n_tile.astype(I32) + (SUBLANES - 1)) & -SUBLANES).astype(F32)
    ur = lax.broadcasted_iota(I32, (LANES, LANES), 0)
    uc = lax.broadcasted_iota(I32, (LANES, LANES), 1)
    upper_tri = jnp.where(ur < uc, 1.0, 0.0).astype(BF16)
    n8 = jnp.broadcast_to(n_tile, (SUBLANES, LANES))
    n_hi, n_lo = _split16(n8)
    off_row = (jnp.dot(n_hi, upper_tri, preferred_element_type=F32)
               + jnp.dot(n_lo, upper_tri, preferred_element_type=F32))[0:1]
    place = before + off_row
    lpos1 = jnp.sum(jnp.where(oh1, place, 0.0), axis=-1, keepdims=True).astype(I32)
    lpos2 = jnp.sum(jnp.where(oh2, place, 0.0), axis=-1, keepdims=True).astype(I32)
    slot = lax.broadcasted_iota(I32, (r, ls), 1)
    perm_t = jnp.where(jnp.logical_or(slot == lpos1, slot == lpos2), 1.0, 0.0).astype(BF16)
    xsl_ref[...] = _dot_tn(perm_t, u_hi)

    cnt = cnt_run_ref[...] + n_tile
    cnt_run_ref[...] = cnt
    cnt_ref[...] = cnt.astype(I32)
    meta_ref[0, 0:1, :] = n_tile.astype(I32)
    meta_ref[0, 1:2, :] = off_row.astype(I32)
    ri_ref[...] = jnp.where(lane == 0, lpos1, jnp.where(lane == 1, lpos2, 0))
    rf_ref[...] = jnp.where(lane == 0, w1, jnp.where(lane == 1, w2, 0.0))


def _mixer(layer, x, mod, lb_logits, w_in16, conv_w, gnorm3, w_out16, ln_g, ln_b, w_r,
           state_conv, state_hgrn, *, bt, tl, chunk, ls):
    nb, seq, d = x.shape
    depth = w_in16.shape[0]
    conv_dim = d // 2
    hg = d - conv_dim
    heads = hg // HEAD_DIM
    r = bt * tl
    has_state = state_conv is not None
    if has_state:
        assert seq == tl == chunk
    n_t = seq // tl
    n_tiles = (nb // bt) * n_t
    const = dict(pipeline_mode=pl.Buffered(1))
    in_specs = [
        pl.BlockSpec((bt, tl, d), lambda b, t: (b, t, 0)),
        pl.BlockSpec((bt, 1, 6 * d), lambda b, t: (b, 0, 0)),
        pl.BlockSpec(lb_logits.shape, lambda b, t: (0, 0), **const),
        pl.BlockSpec((1, d, w_in16.shape[2]), lambda b, t: (layer, 0, 0), **const),
        pl.BlockSpec((1, CONV_W, conv_dim), lambda b, t: (layer, 0, 0), **const),
        pl.BlockSpec((1, 1, HEAD_DIM), lambda b, t: (layer, 0, 0), **const),
        pl.BlockSpec((1, d, d), lambda b, t: (layer, 0, 0), **const),
        pl.BlockSpec((1, 2, d), lambda b, t: (layer, 0, 0), **const),
        pl.BlockSpec((1, 2, d), lambda b, t: (layer, 0, 0), **const),
        pl.BlockSpec((1, d, LANES), lambda b, t: (layer, 0, 0), **const),
    ]
    args = [x, mod, lb_logits, w_in16, conv_w, gnorm3, w_out16, ln_g, ln_b, w_r]
    if has_state:
        in_specs += [
            pl.BlockSpec((1, bt, CONV_W - 1, conv_dim), lambda b, t: (layer, b, 0, 0)),
            pl.BlockSpec((1, bt, heads, HEAD_DIM, HEAD_DIM), lambda b, t: (layer, b, 0, 0, 0)),
        ]
        args += [state_conv, state_hgrn]
    out_shape = (
        jax.ShapeDtypeStruct((nb, seq, d), F32),
        jax.ShapeDtypeStruct((n_tiles * ls, d), F32),
        jax.ShapeDtypeStruct((nb * seq, LANES), I32),
        jax.ShapeDtypeStruct((nb * seq, LANES), F32),
        jax.ShapeDtypeStruct((n_tiles, 2, LANES), I32),
        jax.ShapeDtypeStruct((nb, CONV_W - 1, conv_dim), F32),
        jax.ShapeDtypeStruct((nb, heads, HEAD_DIM, HEAD_DIM), F32),
        jax.ShapeDtypeStruct((1, LANES), I32),
    )
    out_specs = (
        pl.BlockSpec((bt, tl, d), lambda b, t: (b, t, 0)),
        pl.BlockSpec((ls, d), lambda b, t: (b * n_t + t, 0)),
        pl.BlockSpec((r, LANES), lambda b, t: (b * n_t + t, 0)),
        pl.BlockSpec((r, LANES), lambda b, t: (b * n_t + t, 0)),
        pl.BlockSpec((1, 2, LANES), lambda b, t: (b * n_t + t, 0, 0)),
        pl.BlockSpec((bt, CONV_W - 1, conv_dim), lambda b, t: (b, 0, 0)),
        pl.BlockSpec((bt, heads, HEAD_DIM, HEAD_DIM), lambda b, t: (b, 0, 0, 0)),
        pl.BlockSpec((1, LANES), lambda b, t: (0, 0)),
    )
    scratch = [
        pltpu.VMEM((r, d), BF16),
        pltpu.VMEM((heads, HEAD_DIM, HEAD_DIM), F32),
        pltpu.VMEM((1, CONV_W - 1, conv_dim), F32),
        pltpu.VMEM((1, LANES), F32),
    ]
    kern = functools.partial(_mixer_kernel, layer=layer, bt=bt, tl=tl, chunk=chunk, ls=ls,
                             has_state=has_state, alpha=float((2 * depth) ** 0.25))
    return pl.pallas_call(
        kern, grid=(nb // bt, n_t), in_specs=in_specs, out_specs=out_specs, out_shape=out_shape,
        scratch_shapes=scratch,
        compiler_params=pltpu.CompilerParams(
            dimension_semantics=("arbitrary", "arbitrary"), vmem_limit_bytes=VMEM_LIMIT),
        name="mixer_state" if has_state else "mixer_seq",
    )(*args)


def _expert_kernel(blk_e_ref, blk_s_ref, n_used_ref, cnt_ref, meta_ref,
                   xsl_a_ref, xsl_b_ref, wg_ref, wu_ref, wd_ref, yb_ref,
                   xbuf_ref, wg16_ref, wu16_ref, wd16_ref, cur_ref, sem, *, tiles_a, ls_a, tiles_b, ls_b):
    j = pl.program_id(0)
    n_used = n_used_ref[0]
    n_tiles = tiles_a + tiles_b

    def n_valid(jj):
        return jnp.minimum(MOE_BLOCK, cnt_ref[blk_e_ref[jj]] - blk_s_ref[jj])

    def gather(jj, slot):
        e = blk_e_ref[jj]
        s = blk_s_ref[jj]
        end = s + n_valid(jj)

        @pl.when(s == 0)
        def _():
            cur_ref[0] = 0
            cur_ref[1] = 0

        def cond(st):
            pos, tile, _ = st
            return jnp.logical_and(pos < end, tile < n_tiles)

        def body(st):
            pos, tile, c = st
            n = meta_ref[tile * TILE_META + e]
            off = meta_ref[tile * TILE_META + LANES + e]
            stop = jnp.minimum(c + n, end)
            src = off + (pos - c)

            @pl.when(jnp.logical_and(stop > pos, tile < tiles_a))
            def _():
                _copy_run(xsl_a_ref, xbuf_ref.at[slot], tile * ls_a + src, pos - s, stop - pos,
                          sem.at[slot], MOE_BLOCK, False)

            @pl.when(jnp.logical_and(stop > pos, tile >= tiles_a))
            def _():
                _copy_run(xsl_b_ref, xbuf_ref.at[slot], (tile - tiles_a) * ls_b + src, pos - s, stop - pos,
                          sem.at[slot], MOE_BLOCK, False)

            used_up = (c + n) <= end
            return (jnp.maximum(pos, stop), tile + jnp.where(used_up, 1, 0), c + jnp.where(used_up, n, 0))

        _, tile, c = lax.while_loop(cond, body, (s, cur_ref[0], cur_ref[1]))
        cur_ref[0] = tile
        cur_ref[1] = c

    @pl.when(j == 0)
    def _():
        xbuf_ref[...] = jnp.zeros_like(xbuf_ref)
        gather(0, 0)

    @pl.when(j + 1 < n_used)
    def _():
        gather(j + 1, (j + 1) & 1)

    @pl.when(j >= n_used)
    def _():
        yb_ref[...] = jnp.zeros_like(yb_ref)

    @pl.when(j < n_used)
    def _():
        _wait_rows(xsl_a_ref, xbuf_ref.at[j & 1], n_valid(j), sem.at[j & 1], MOE_BLOCK)
        changed = jnp.logical_or(j == 0, blk_e_ref[j] != blk_e_ref[jnp.maximum(j - 1, 0)])

        @pl.when(changed)
        def _():
            wg16_ref[...] = wg_ref[0, 0].astype(BF16)
            wu16_ref[...] = wu_ref[0, 0].astype(BF16)
            wd16_ref[...] = wd_ref[0, 0].astype(BF16)

        x = xbuf_ref[j & 1].astype(BF16)
        g = jnp.dot(x, wg16_ref[...], preferred_element_type=F32)
        up = jnp.dot(x, wu16_ref[...], preferred_element_type=F32)
        hid = (g * (0.5 + 0.5 * jnp.tanh(0.5 * g)) * up).astype(BF16)
        yb_ref[...] = jnp.dot(hid, wd16_ref[...], preferred_element_type=F32)


def _experts(layer, xsl_a, xsl_b, blk_e, blk_s, n_used, cnt, meta, w_gate, w_up, w_down, *, ls_a, ls_b):
    d = xsl_a.shape[1]
    de = w_gate.shape[-1]
    n_blk = blk_e.shape[0]

    def w_idx(j, be, bs, nu, ct, mt):
        return (layer, be[jnp.minimum(j, nu[0] - 1)], 0, 0)

    grid_spec = pltpu.PrefetchScalarGridSpec(
        num_scalar_prefetch=5,
        grid=(n_blk,),
        in_specs=[
            pl.BlockSpec(memory_space=pl.ANY),
            pl.BlockSpec(memory_space=pl.ANY),
            pl.BlockSpec((1, 1, d, de), w_idx),
            pl.BlockSpec((1, 1, d, de), w_idx),
            pl.BlockSpec((1, 1, de, d), w_idx),
        ],
        out_specs=pl.BlockSpec((MOE_BLOCK, d), lambda j, *a: (j, 0)),
        scratch_shapes=[
            pltpu.VMEM((2, MOE_BLOCK, d), F32),
            pltpu.VMEM((d, de), BF16),
            pltpu.VMEM((d, de), BF16),
            pltpu.VMEM((de, d), BF16),
            pltpu.SMEM((2,), I32),
            pltpu.SemaphoreType.DMA((2,)),
        ],
    )
    return pl.pallas_call(
        functools.partial(_expert_kernel, tiles_a=xsl_a.shape[0] // ls_a, ls_a=ls_a,
                          tiles_b=xsl_b.shape[0] // ls_b, ls_b=ls_b),
        grid_spec=grid_spec,
        out_shape=jax.ShapeDtypeStruct((n_blk * MOE_BLOCK, d), F32),
        compiler_params=pltpu.CompilerParams(
            dimension_semantics=("arbitrary",), vmem_limit_bytes=VMEM_LIMIT),
        name="moe_experts",
    )(blk_e, blk_s, n_used, cnt, meta, xsl_a, xsl_b, w_gate, w_up, w_down)


def _combine_kernel(cnt_ref, cur0_ref, meta_ref, yb_ref, x1_ref, ri_ref, rf_ref, mod_ref, lng_ref, lnb_ref,
                    o_ref, ybuf_ref, cur_ref, sem, *, alpha, n_t, tile0):
    bt, tl, d = x1_ref.shape
    ls = ybuf_ref.shape[1]
    ls_pow2 = 1 << (ls.bit_length() - 1)
    r = bt * tl
    step = pl.program_id(0) * n_t + pl.program_id(1)
    n_steps = pl.num_programs(0) * n_t

    def gather(i, slot):
        def body(e, carry):
            n = meta_ref[(tile0 + i) * TILE_META + e]
            off = meta_ref[(tile0 + i) * TILE_META + LANES + e]
            _copy_run(yb_ref, ybuf_ref.at[slot], cur_ref[e], off, n, sem.at[slot], r, False)
            cur_ref[e] = cur_ref[e] + n
            return carry

        lax.fori_loop(0, N_EXPERTS, body, 0)

    @pl.when(step == 0)
    def _():
        def init(e, acc):
            cur_ref[e] = acc + cur0_ref[e]
            return acc + ((cnt_ref[e] + MOE_BLOCK - 1) // MOE_BLOCK) * MOE_BLOCK

        lax.fori_loop(0, N_EXPERTS, init, jnp.int32(0))
        ybuf_ref[...] = jnp.zeros_like(ybuf_ref)
        gather(0, 0)

    @pl.when(step + 1 < n_steps)
    def _():
        gather(step + 1, (step + 1) & 1)

    last = (tile0 + step) * TILE_META + N_EXPERTS - 1
    n_rows = meta_ref[last] + meta_ref[last + LANES]
    _wait_rows(yb_ref, ybuf_ref.at[step & 1], n_rows, sem.at[step & 1], ls_pow2)
    ys = ybuf_ref[step & 1].astype(BF16)
    ri = ri_ref[...]
    rf = rf_ref[...]
    slot = lax.broadcasted_iota(I32, (r, ls), 1)
    sel1 = jnp.where(slot == ri[:, 0:1], 1.0, 0.0).astype(BF16)
    sel2 = jnp.where(slot == ri[:, 1:2], 1.0, 0.0).astype(BF16)
    y = (rf[:, 0:1] * jnp.dot(sel1, ys, preferred_element_type=F32)
         + rf[:, 1:2] * jnp.dot(sel2, ys, preferred_element_type=F32))
    gate2 = mod_ref[:, :, 5 * d:6 * d]
    z = alpha * x1_ref[...] + gate2 * y.reshape(bt, tl, d)
    o_ref[...] = _layer_norm(z, lng_ref[0, 1:2, :], lnb_ref[0, 1:2, :])


def _combine(layer, cnt, cur0, meta, yb, x1, ri, rf, mod, ln_g, ln_b, *, bt, tl, ls, tile0, alpha):
    nb, seq, d = x1.shape
    r = bt * tl
    n_t = seq // tl
    grid_spec = pltpu.PrefetchScalarGridSpec(
        num_scalar_prefetch=3,
        grid=(nb // bt, n_t),
        in_specs=[
            pl.BlockSpec(memory_space=pl.ANY),
            pl.BlockSpec((bt, tl, d), lambda b, t, *a: (b, t, 0)),
            pl.BlockSpec((r, LANES), lambda b, t, *a: (b * n_t + t, 0)),
            pl.BlockSpec((r, LANES), lambda b, t, *a: (b * n_t + t, 0)),
            pl.BlockSpec((bt, 1, 6 * d), lambda b, t, *a: (b, 0, 0)),
            pl.BlockSpec((1, 2, d), lambda b, t, *a: (layer, 0, 0)),
            pl.BlockSpec((1, 2, d), lambda b, t, *a: (layer, 0, 0)),
        ],
        out_specs=pl.BlockSpec((bt, tl, d), lambda b, t, *a: (b, t, 0)),
        scratch_shapes=[
            pltpu.VMEM((2, ls, d), F32),
            pltpu.SMEM((N_EXPERTS,), I32),
            pltpu.SemaphoreType.DMA((2,)),
        ],
    )
    return pl.pallas_call(
        functools.partial(_combine_kernel, alpha=alpha, n_t=n_t, tile0=tile0),
        grid_spec=grid_spec,
        out_shape=jax.ShapeDtypeStruct((nb, seq, d), F32),
        compiler_params=pltpu.CompilerParams(
            dimension_semantics=("arbitrary", "arbitrary"), vmem_limit_bytes=VMEM_LIMIT),
        name="moe_combine",
    )(cnt, cur0, meta, yb, x1, ri, rf, mod, ln_g, ln_b)


def _moe_experts(layer, xsl_a, xsl_b, meta_a, meta_b, cnt_a, cnt_b, w_gate, w_up, w_down, *, ls_a, ls_b):
    tiles_a = xsl_a.shape[0] // ls_a
    tiles_b = xsl_b.shape[0] // ls_b
    n_rows = xsl_a.shape[0] + xsl_b.shape[0]
    n_blk = -(-(n_rows + N_EXPERTS * (MOE_BLOCK - 1)) // MOE_BLOCK)
    cnt = cnt_a[0] + cnt_b[0]
    padded = (cnt[:N_EXPERTS] + MOE_BLOCK - 1) // MOE_BLOCK * MOE_BLOCK
    pad_end = jnp.cumsum(padded).astype(I32)
    blk_start = jnp.arange(n_blk, dtype=I32) * MOE_BLOCK
    before = pad_end[None, :] <= blk_start[:, None]
    blk_e = jnp.minimum(jnp.sum(before, axis=1), N_EXPERTS - 1).astype(I32)
    blk_s = blk_start - jnp.sum(jnp.where(before, padded[None, :], 0), axis=1).astype(I32)
    n_used = pad_end[N_EXPERTS - 1:] // MOE_BLOCK
    meta = jnp.concatenate([meta_a.reshape(tiles_a * TILE_META), meta_b.reshape(tiles_b * TILE_META)])
    yb = _experts(layer, xsl_a, xsl_b, blk_e, blk_s, n_used, cnt, meta, w_gate, w_up, w_down,
                  ls_a=ls_a, ls_b=ls_b)
    return yb, cnt, meta


def _slot_capacity(tokens):
    return -(-(2 * tokens + RUN_PAD) // MOE_BLOCK) * MOE_BLOCK


def kernel(x_prompt, x_sample, state_conv, state_hgrn, c_prompt, c_sample, hgrn_lb_logits,
           w_mod, b_mod, w_in, conv_w, gnorm_w, w_out, ln_g, ln_b,
           w_router_group, w_router_expert, w_e_gate, w_e_up, w_e_down):
    depth, d, _ = w_in.shape
    nb_p, seq_p, _ = x_prompt.shape
    nb_s, seq_s, _ = x_sample.shape
    alpha = float((2 * depth) ** 0.25)

    mod_all = _modulation(jnp.concatenate([c_prompt, c_sample], axis=0), w_mod, b_mod)
    mod_p = mod_all[:, :nb_p].reshape(depth, nb_p, 1, 6 * d)
    mod_s = mod_all[:, nb_p:].reshape(depth, nb_s, 1, 6 * d)

    w_in16 = w_in.astype(BF16)
    w_out16 = w_out.astype(BF16)
    gnorm3 = gnorm_w.reshape(depth, 1, HEAD_DIM)
    w_r = jnp.concatenate(
        [w_router_expert, w_router_group,
         jnp.zeros((depth, d, LANES - N_EXPERTS - N_GROUPS), F32)], axis=-1)

    tl_p = min(512, seq_p)
    bt_s = 16
    ls_p = _slot_capacity(tl_p)
    ls_s = _slot_capacity(bt_s * seq_s)
    yp, ys = x_prompt, x_sample
    conv_p, hgrn_p, conv_s, hgrn_s = [], [], [], []
    for l in range(depth):
        x1_p, xsl_p, ri_p, rf_p, meta_p, cb, sb, cnt_p = _mixer(
            l, yp, mod_p[l], hgrn_lb_logits, w_in16, conv_w, gnorm3, w_out16, ln_g, ln_b, w_r,
            None, None, bt=1, tl=tl_p, chunk=64, ls=ls_p)
        x1_s, xsl_s, ri_s, rf_s, meta_s, cs, ss, cnt_s = _mixer(
            l, ys, mod_s[l], hgrn_lb_logits, w_in16, conv_w, gnorm3, w_out16, ln_g, ln_b, w_r,
            state_conv, state_hgrn, bt=bt_s, tl=seq_s, chunk=seq_s, ls=ls_s)
        yb, cnt, meta = _moe_experts(l, xsl_p, xsl_s, meta_p, meta_s, cnt_p, cnt_s,
                                     w_e_gate, w_e_up, w_e_down, ls_a=ls_p, ls_b=ls_s)
        yp = _combine(l, cnt, jnp.zeros_like(cnt), meta, yb, x1_p, ri_p, rf_p, mod_p[l], ln_g, ln_b,
                      bt=1, tl=tl_p, ls=ls_p, tile0=0, alpha=alpha)
        ys = _combine(l, cnt, cnt_p[0], meta, yb, x1_s, ri_s, rf_s, mod_s[l], ln_g, ln_b,
                      bt=bt_s, tl=seq_s, ls=ls_s, tile0=meta_p.shape[0], alpha=alpha)
        conv_p.append(cb)
        hgrn_p.append(sb)
        conv_s.append(cs)
        hgrn_s.append(ss)
    return (yp, ys, jnp.stack(conv_p), jnp.stack(hgrn_p), jnp.stack(conv_s), jnp.stack(hgrn_s))
```

```python
import functools

import jax
import jax.numpy as jnp
from jax import lax
from jax.experimental import pallas as pl
from jax.experimental.pallas import tpu as pltpu

F32 = jnp.float32
BF16 = jnp.bfloat16
I32 = jnp.int32

CONV_W = 3
HEAD_DIM = 128
N_GROUPS = 4
EXPERTS_PER_GROUP = 8
N_EXPERTS = N_GROUPS * EXPERTS_PER_GROUP
LN_EPS = 1e-5
RMS_EPS = 1e-6
LANES = 128
SUBLANES = 8
MOE_BLOCK = 256
TILE_META = 2 * LANES
VMEM_LIMIT = 56 * 1024 * 1024
NEG = -1e30
RUN_PAD = N_EXPERTS * (SUBLANES - 1)
LOG2E = 1.4426950408889634


def _seg_row(x, seg, row):
    r, n = x.shape
    x3 = x.reshape(r // seg, seg, n)
    return jnp.broadcast_to(x3[:, row:row + 1, :], x3.shape).reshape(r, n)


def _roll8(x, shift):
    r, n = x.shape
    return pltpu.roll(x.reshape(r // SUBLANES, SUBLANES, n), shift, 1).reshape(r, n)


def _dot_nt(a, b):
    return lax.dot_general(a, b, (((1,), (1,)), ((), ())), preferred_element_type=F32)


def _dot_tn(a, b):
    return lax.dot_general(a, b, (((0,), (0,)), ((), ())), preferred_element_type=F32)


def _split16(x):
    hi = x.astype(BF16)
    return hi, (x - hi.astype(F32)).astype(BF16)


def _layer_norm(z, g, b):
    mu = jnp.mean(z, axis=-1, keepdims=True)
    zc = z - mu
    var = jnp.mean(zc * zc, axis=-1, keepdims=True)
    return zc * lax.rsqrt(var + LN_EPS) * g + b


def _copy_run(src_ref, dst_ref, src0, dst0, n, sem, max_rows, wait):
    off = jnp.int32(0)
    p = max_rows
    while p >= SUBLANES:
        @pl.when((n & p) != 0)
        def _(off=off, p=p):
            cp = pltpu.make_async_copy(
                src_ref.at[pl.ds(pl.multiple_of(src0 + off, SUBLANES), p)],
                dst_ref.at[pl.ds(pl.multiple_of(dst0 + off, SUBLANES), p)], sem)
            if wait:
                cp.wait()
            else:
                cp.start()
        off = off + (n & p)
        p //= 2


def _wait_rows(src_ref, dst_ref, n, sem, max_rows):
    p = max_rows
    while p >= SUBLANES:
        @pl.when((n & p) != 0)
        def _(p=p):
            pltpu.make_async_copy(src_ref.at[pl.ds(0, p)], dst_ref.at[pl.ds(0, p)], sem).wait()
        p //= 2


def _mod_kernel(c_ref, w_ref, b_ref, o_ref):
    c = c_ref[...]
    a = (c * (0.5 + 0.5 * jnp.tanh(0.5 * c))).astype(BF16)
    o_ref[0] = jnp.dot(a, w_ref[0].astype(BF16), preferred_element_type=F32) + b_ref[0]


def _modulation(c_all, w_mod, b_mod):
    depth, d, d6 = w_mod.shape
    nb = c_all.shape[0]
    n_col = d6 // d
    return pl.pallas_call(
        _mod_kernel,
        grid=(depth, n_col),
        in_specs=[
            pl.BlockSpec((nb, d), lambda l, j: (0, 0)),
            pl.BlockSpec((1, d, d), lambda l, j: (l, 0, j)),
            pl.BlockSpec((1, 1, d), lambda l, j: (l, 0, j)),
        ],
        out_specs=pl.BlockSpec((1, nb, d), lambda l, j: (l, 0, j)),
        out_shape=jax.ShapeDtypeStruct((depth, nb, d6), F32),
        compiler_params=pltpu.CompilerParams(
            dimension_semantics=("arbitrary", "arbitrary"), vmem_limit_bytes=VMEM_LIMIT),
        name="modulation",
    )(c_all, w_mod, b_mod.reshape(depth, 1, d6))


def _mixer_kernel(*refs, layer, bt, tl, chunk, ls, n_t, n_tiles, has_state, alpha):
    if has_state:
        (x_ref, mod_ref, xp_ref, modp_ref, lbl_ref, w_in_ref, cw_ref, gn_ref, w_out_ref, lng_ref, lnb_ref,
         wr_ref, conv0_ref, s0_ref,
         x1_ref, xsl_ref, ri_ref, rf_ref, meta_ref, conv_out_ref, s_out_ref, cnt_ref,
         ymix_ref, s_run_ref, tail_ref, cnt_run_ref) = refs
    else:
        (x_ref, mod_ref, xp_ref, modp_ref, lbl_ref, w_in_ref, cw_ref, gn_ref, w_out_ref, lng_ref, lnb_ref,
         wr_ref,
         x1_ref, xsl_ref, ri_ref, rf_ref, meta_ref, conv_out_ref, s_out_ref, cnt_ref,
         ymix_ref, s_run_ref, tail_ref, cnt_run_ref) = refs
    d = x_ref.shape[-1]
    conv_dim = d // 2
    hg = d - conv_dim
    heads = hg // HEAD_DIM
    r = bt * tl
    n_units = r // chunk
    step_id = pl.program_id(0)
    is_real = step_id < n_tiles
    has_prev = step_id > 0

    @pl.when(step_id == 0)
    def _():
        cnt_run_ref[...] = jnp.zeros_like(cnt_run_ref)
        ymix_ref[...] = jnp.zeros_like(ymix_ref)

    if not has_state:
        @pl.when(jnp.logical_and(is_real, step_id % n_t == 0))
        def _():
            s_run_ref[...] = jnp.zeros_like(s_run_ref)
            tail_ref[...] = jnp.zeros_like(tail_ref)

    xp3 = xp_ref[...]
    modp3 = modp_ref[...]
    gate1, shift2, scale2 = (modp3[:, :, k * d:(k + 1) * d] for k in range(2, 5))
    mix = jnp.dot(ymix_ref[...], w_out_ref[0], preferred_element_type=F32).reshape(bt, tl, d)

    lbl = lbl_ref[...]
    lbe = jnp.exp(lbl - jnp.max(lbl, axis=0, keepdims=True))
    lbp = lbe / jnp.sum(lbe, axis=0, keepdims=True)
    lb_all = lbp[0:1] * 0.0
    for j in range(1, layer + 1):
        lb_all = lb_all + lbp[j:j + 1]

    x3 = x_ref[...]
    mod3 = mod_ref[...]
    shift1, scale1 = (mod3[:, :, k * d:(k + 1) * d] for k in range(2))
    u = (x3 * (1.0 + scale1) + shift1).reshape(r, d).astype(BF16)

    def proj(k, width):
        return jnp.dot(u, w_in_ref[0, :, k:k + width], preferred_element_type=F32)

    conv = {}

    def conv_b():
        conv["b"] = proj(0, conv_dim)

    def conv_c():
        conv["c"] = proj(conv_dim, conv_dim)

    def conv_v():
        v = conv["c"] * proj(2 * conv_dim, conv_dim)
        prev = conv0_ref[0] if has_state else tail_ref[...]
        rows_c = lax.broadcasted_iota(I32, (r, conv_dim), 0)
        pos_t = rows_c & (tl - 1)
        prev1 = jnp.broadcast_to(prev[:, 1:2, :], (bt, tl, conv_dim)).reshape(r, conv_dim)
        prev0 = jnp.broadcast_to(prev[:, 0:1, :], (bt, tl, conv_dim)).reshape(r, conv_dim)
        v1 = jnp.where(pos_t >= 1, pltpu.roll(v, 1, 0), prev1)
        v2 = jnp.where(pos_t >= 2, pltpu.roll(v, 2, 0), jnp.where(pos_t == 1, prev1, prev0))
        cw = cw_ref[0]
        y_conv = conv["b"] * (cw[0:1] * v2 + cw[1:2] * v1 + cw[2:3] * v)
        ymix_ref[:, 0:conv_dim] = y_conv.astype(BF16)
        new_tail = v.reshape(bt, tl, conv_dim)[:, tl - 2:tl, :]
        if not has_state:
            new_tail = jnp.where(is_real, new_tail, prev)
            tail_ref[...] = new_tail
        conv_out_ref[...] = new_tail

    rows = lax.broadcasted_iota(I32, (r, HEAD_DIM), 0)
    pos_c = rows & (chunk - 1)
    pos8 = rows & (SUBLANES - 1)
    gn = gn_ref[0]
    t_i = lax.broadcasted_iota(I32, (chunk, chunk), 0)
    s_i = lax.broadcasted_iota(I32, (chunk, chunk), 1)
    level_masks = []
    m = SUBLANES
    while m < chunk:
        sh = m.bit_length()
        same = (t_i >> sh) == (s_i >> sh)
        level_masks.append((m, jnp.where(same, jnp.where(
            (t_i & (2 * m - 1)) >= m, jnp.where((s_i & (2 * m - 1)) < m, 1.0, 0.0), 0.0), 0.0)))
        m *= 2
    lane = lax.broadcasted_iota(I32, (r, LANES), 1)
    st = {}

    def finish_norm():
        x1 = _layer_norm(alpha * xp3 + gate1 * mix, lng_ref[0, 0:1, :], lnb_ref[0, 0:1, :])
        x1_ref[...] = x1
        u2 = (x1 * (1.0 + scale2) + shift2).reshape(r, d)
        st["u_hi"], st["u_lo"] = _split16(u2)

    def finish_logits():
        w_hi, w_lo = _split16(wr_ref[0])
        st["logits"] = (jnp.dot(st["u_hi"], w_hi, preferred_element_type=F32)
                        + jnp.dot(st["u_lo"], w_hi, preferred_element_type=F32)
                        + jnp.dot(st["u_hi"], w_lo, preferred_element_type=F32))

    def finish_route():
        logits = st["logits"]
        is_g = jnp.logical_and(lane >= N_EXPERTS, lane < N_EXPERTS + N_GROUPS)
        gl = jnp.where(is_g, logits, NEG)
        gm = jnp.max(gl, axis=-1, keepdims=True)
        grp = jnp.min(jnp.where(gl == gm, lane - N_EXPERTS, LANES), axis=-1, keepdims=True)
        p_sel = 1.0 / jnp.sum(jnp.where(is_g, jnp.exp(gl - gm), 0.0), axis=-1, keepdims=True)
        in_grp = jnp.logical_and(lane < N_EXPERTS, (lane >> 3) == grp)
        el = jnp.where(in_grp, logits, NEG)
        m1 = jnp.max(el, axis=-1, keepdims=True)
        i1 = jnp.min(jnp.where(jnp.logical_and(in_grp, el == m1), lane, LANES), axis=-1, keepdims=True)
        in2 = jnp.logical_and(in_grp, lane != i1)
        el2 = jnp.where(in2, logits, NEG)
        m2 = jnp.max(el2, axis=-1, keepdims=True)
        i2 = jnp.min(jnp.where(jnp.logical_and(in2, el2 == m2), lane, LANES), axis=-1, keepdims=True)
        e21 = jnp.exp(m2 - m1)
        rf_ref[...] = jnp.where(lane == 0, p_sel / (1.0 + e21),
                                jnp.where(lane == 1, p_sel * e21 / (1.0 + e21), 0.0))
        st["oh1"] = lane == i1
        st["oh2"] = lane == i2
        ohs = jnp.where(st["oh1"], 1.0, 0.0) + jnp.where(st["oh2"], 1.0, 0.0)
        tr = lax.broadcasted_iota(I32, (r, r), 0)
        tc = lax.broadcasted_iota(I32, (r, r), 1)
        ltri = jnp.where(tr > tc, 1.0, 0.0).astype(BF16)
        st["before"] = jnp.dot(ltri, ohs.astype(BF16), preferred_element_type=F32)
        st["n_tile"] = jnp.sum(ohs, axis=0, keepdims=True)

    def finish_sort():
        n_tile = ((st["n_tile"].astype(I32) + (SUBLANES - 1)) & -SUBLANES).astype(F32)
        ur = lax.broadcasted_iota(I32, (LANES, LANES), 0)
        uc = lax.broadcasted_iota(I32, (LANES, LANES), 1)
        upper_tri = jnp.where(ur < uc, 1.0, 0.0).astype(BF16)
        n_hi, n_lo = _split16(jnp.broadcast_to(n_tile, (SUBLANES, LANES)))
        off_row = (jnp.dot(n_hi, upper_tri, preferred_element_type=F32)
                   + jnp.dot(n_lo, upper_tri, preferred_element_type=F32))[0:1]
        place = st["before"] + off_row
        lpos1 = jnp.sum(jnp.where(st["oh1"], place, 0.0), axis=-1, keepdims=True).astype(I32)
        lpos2 = jnp.sum(jnp.where(st["oh2"], place, 0.0), axis=-1, keepdims=True).astype(I32)
        slot = lax.broadcasted_iota(I32, (r, ls), 1)
        perm_t = jnp.where(jnp.logical_or(slot == lpos1, slot == lpos2), 1.0, 0.0).astype(BF16)
        xsl_ref[...] = _dot_tn(perm_t, st["u_hi"])
        cnt = cnt_run_ref[...] + jnp.where(has_prev, n_tile, 0.0)
        cnt_run_ref[...] = cnt
        cnt_ref[...] = cnt.astype(I32)
        meta_ref[0, 0:1, :] = n_tile.astype(I32)
        meta_ref[0, 1:2, :] = off_row.astype(I32)
        ri_ref[...] = jnp.where(lane == 0, lpos1, jnp.where(lane == 1, lpos2, 0))

    q_off = 3 * conv_dim
    q_all = proj(q_off, hg)
    finish_norm()
    f_all = proj(q_off + hg, hg)
    i_all = proj(q_off + 2 * hg, hg)
    late = {}

    def proj_g():
        late["g"] = proj(q_off + 3 * hg, hg)

    pending = [(finish_logits, proj_g), (conv_b, finish_route), (conv_c, finish_sort), (conv_v,)]
    for h in range(heads):
        lo = h * HEAD_DIM
        hs = slice(lo, lo + HEAD_DIM)
        lb = lb_all[:, hs]
        qz, fz, vv = q_all[:, hs], f_all[:, hs], i_all[:, hs]
        qq = qz * (0.5 + 0.5 * jnp.tanh(0.5 * qz)) * (HEAD_DIM ** -0.5)
        tf = jnp.tanh(0.5 * fz)
        lf = jnp.log2(lb + (1.0 - lb) * (0.5 + 0.5 * tf))
        kk = (1.0 - lb) * (0.5 - 0.5 * tf)
        b = lf
        step = 1
        while step < chunk:
            b = b + jnp.where(pos_c >= step, pltpu.roll(b, step, 0), 0.0)
            step *= 2
        if h < len(pending):
            for stage in pending[h]:
                stage()
        gz = late["g"][:, hs]
        o = jnp.sum(qq * kk, axis=-1, keepdims=True) * vv
        for dd in range(1, SUBLANES):
            e = qq * _roll8(kk, dd) * jnp.exp2(b - _roll8(b, dd))
            a = jnp.sum(e, axis=-1, keepdims=True)
            o = o + jnp.where(pos8 >= dd, a * _roll8(vv, dd), 0.0)
        vv16 = vv.astype(BF16)
        o_units = [None] * n_units
        if chunk > SUBLANES:
            a_units = [None] * n_units
            for m, mask in level_masks:
                br = _seg_row(b, 2 * m, m - 1)
                y = jnp.exp2(jnp.minimum(jnp.where((rows & (2 * m - 1)) >= m, b - br, br - b), 0.0))
                qm = (qq * y).astype(BF16)
                km = (kk * y).astype(BF16)
                for c in range(n_units):
                    sl = slice(c * chunk, (c + 1) * chunk)
                    part = mask * _dot_nt(qm[sl], km[sl])
                    a_units[c] = part if a_units[c] is None else a_units[c] + part
            for c in range(n_units):
                sl = slice(c * chunk, (c + 1) * chunk)
                o_units[c] = jnp.dot(a_units[c].astype(BF16), vv16[sl], preferred_element_type=F32)
        bend = _seg_row(b, chunk, chunk - 1)
        qt = (qq * jnp.exp2(b)).astype(BF16)
        kh = (kk * jnp.exp2(bend - b)).astype(BF16)
        bl = b.reshape(n_units, chunk, HEAD_DIM)[:, chunk - 1, :]
        bl = jnp.concatenate([bl, jnp.zeros((HEAD_DIM - n_units, HEAD_DIM), F32)], axis=0)
        dec_t = jnp.exp2(bl.T)
        if not has_state:
            s_cur = s_run_ref[h]
        for c in range(n_units):
            sl = slice(c * chunk, (c + 1) * chunk)
            if has_state:
                s_cur = s0_ref[0, c, h]
            o_s = jnp.dot(qt[sl], s_cur.astype(BF16), preferred_element_type=F32)
            o_units[c] = o_s if o_units[c] is None else o_units[c] + o_s
            s_new = dec_t[:, c:c + 1] * s_cur + _dot_tn(kh[sl], vv16[sl])
            if has_state:
                s_out_ref[c, h] = s_new
            else:
                s_cur = s_new
        if not has_state:
            s_cur = jnp.where(is_real, s_cur, s_run_ref[h])
            s_run_ref[h] = s_cur
            s_out_ref[0, h] = s_cur
        o = o + jnp.concatenate(o_units, axis=0)
        o = o * lax.rsqrt(jnp.mean(o * o, axis=-1, keepdims=True) + RMS_EPS)
        o = o * gn * (gz * (0.5 + 0.5 * jnp.tanh(0.5 * gz)))
        ymix_ref[:, conv_dim + lo:conv_dim + lo + HEAD_DIM] = o.astype(BF16)


def _mixer(layer, x, mod, lb_logits, w_in16, conv_w, gnorm3, w_out16, ln_g, ln_b, w_r,
           state_conv, state_hgrn, *, bt, tl, chunk, ls):
    nb, seq, d = x.shape
    depth = w_in16.shape[0]
    conv_dim = d // 2
    hg = d - conv_dim
    heads = hg // HEAD_DIM
    r = bt * tl
    has_state = state_conv is not None
    if has_state:
        assert seq == tl == chunk
    n_t = seq // tl
    n_tiles = (nb // bt) * n_t
    const = dict(pipeline_mode=pl.Buffered(1))

    def now(s):
        return jnp.minimum(s, n_tiles - 1)

    def prev(s):
        return jnp.maximum(s - 1, 0)

    in_specs = [
        pl.BlockSpec((bt, tl, d), lambda s: (now(s) // n_t, now(s) % n_t, 0)),
        pl.BlockSpec((bt, 1, 6 * d), lambda s: (now(s) // n_t, 0, 0)),
        pl.BlockSpec((bt, tl, d), lambda s: (prev(s) // n_t, prev(s) % n_t, 0)),
        pl.BlockSpec((bt, 1, 6 * d), lambda s: (prev(s) // n_t, 0, 0)),
        pl.BlockSpec(lb_logits.shape, lambda s: (0, 0), **const),
        pl.BlockSpec((1, d, w_in16.shape[2]), lambda s: (layer, 0, 0), **const),
        pl.BlockSpec((1, CONV_W, conv_dim), lambda s: (layer, 0, 0), **const),
        pl.BlockSpec((1, 1, HEAD_DIM), lambda s: (layer, 0, 0), **const),
        pl.BlockSpec((1, d, d), lambda s: (layer, 0, 0), **const),
        pl.BlockSpec((1, 2, d), lambda s: (layer, 0, 0), **const),
        pl.BlockSpec((1, 2, d), lambda s: (layer, 0, 0), **const),
        pl.BlockSpec((1, d, LANES), lambda s: (layer, 0, 0), **const),
    ]
    args = [x, mod, x, mod, lb_logits, w_in16, conv_w, gnorm3, w_out16, ln_g, ln_b, w_r]
    if has_state:
        in_specs += [
            pl.BlockSpec((1, bt, CONV_W - 1, conv_dim), lambda s: (layer, now(s), 0, 0)),
            pl.BlockSpec((1, bt, heads, HEAD_DIM, HEAD_DIM), lambda s: (layer, now(s), 0, 0, 0)),
        ]
        args += [state_conv, state_hgrn]
    out_shape = (
        jax.ShapeDtypeStruct((nb, seq, d), F32),
        jax.ShapeDtypeStruct((n_tiles * ls, d), F32),
        jax.ShapeDtypeStruct((nb * seq, LANES), I32),
        jax.ShapeDtypeStruct((nb * seq, LANES), F32),
        jax.ShapeDtypeStruct((n_tiles, 2, LANES), I32),
        jax.ShapeDtypeStruct((nb, CONV_W - 1, conv_dim), F32),
        jax.ShapeDtypeStruct((nb, heads, HEAD_DIM, HEAD_DIM), F32),
        jax.ShapeDtypeStruct((1, LANES), I32),
    )
    out_specs = (
        pl.BlockSpec((bt, tl, d), lambda s: (prev(s) // n_t, prev(s) % n_t, 0)),
        pl.BlockSpec((ls, d), lambda s: (prev(s), 0)),
        pl.BlockSpec((r, LANES), lambda s: (prev(s), 0)),
        pl.BlockSpec((r, LANES), lambda s: (prev(s), 0)),
        pl.BlockSpec((1, 2, LANES), lambda s: (prev(s), 0, 0)),
        pl.BlockSpec((bt, CONV_W - 1, conv_dim), lambda s: (now(s) // n_t, 0, 0)),
        pl.BlockSpec((bt, heads, HEAD_DIM, HEAD_DIM), lambda s: (now(s) // n_t, 0, 0, 0)),
        pl.BlockSpec((1, LANES), lambda s: (0, 0)),
    )
    scratch = [
        pltpu.VMEM((r, d), BF16),
        pltpu.VMEM((heads, HEAD_DIM, HEAD_DIM), F32),
        pltpu.VMEM((1, CONV_W - 1, conv_dim), F32),
        pltpu.VMEM((1, LANES), F32),
    ]
    kern = functools.partial(_mixer_kernel, layer=layer, bt=bt, tl=tl, chunk=chunk, ls=ls, n_t=n_t,
                             n_tiles=n_tiles, has_state=has_state, alpha=float((2 * depth) ** 0.25))
    return pl.pallas_call(
        kern, grid=(n_tiles + 1,), in_specs=in_specs, out_specs=out_specs, out_shape=out_shape,
        scratch_shapes=scratch,
        compiler_params=pltpu.CompilerParams(
            dimension_semantics=("arbitrary",), vmem_limit_bytes=VMEM_LIMIT),
        name="mixer_state" if has_state else "mixer_seq",
    )(*args)


def _expert_kernel(blk_e_ref, blk_s_ref, n_used_ref, cnt_ref, meta_ref,
                   xsl_a_ref, xsl_b_ref, wg_ref, wu_ref, wd_ref, yb_ref,
                   xbuf_ref, wg16_ref, wu16_ref, wd16_ref, cur_ref, sem, *, tiles_a, ls_a, tiles_b, ls_b):
    j = pl.program_id(0)
    n_used = n_used_ref[0]
    n_tiles = tiles_a + tiles_b

    def n_valid(jj):
        return jnp.minimum(MOE_BLOCK, cnt_ref[blk_e_ref[jj]] - blk_s_ref[jj])

    def gather(jj, slot):
        e = blk_e_ref[jj]
        s = blk_s_ref[jj]
        end = s + n_valid(jj)

        @pl.when(s == 0)
        def _():
            cur_ref[0] = 0
            cur_ref[1] = 0

        def cond(st):
            pos, tile, _ = st
            return jnp.logical_and(pos < end, tile < n_tiles)

        def body(st):
            pos, tile, c = st
            n = meta_ref[tile * TILE_META + e]
            off = meta_ref[tile * TILE_META + LANES + e]
            stop = jnp.minimum(c + n, end)
            src = off + (pos - c)

            @pl.when(jnp.logical_and(stop > pos, tile < tiles_a))
            def _():
                _copy_run(xsl_a_ref, xbuf_ref.at[slot], tile * ls_a + src, pos - s, stop - pos,
                          sem.at[slot], MOE_BLOCK, False)

            @pl.when(jnp.logical_and(stop > pos, tile >= tiles_a))
            def _():
                _copy_run(xsl_b_ref, xbuf_ref.at[slot], (tile - tiles_a) * ls_b + src, pos - s, stop - pos,
                          sem.at[slot], MOE_BLOCK, False)

            used_up = (c + n) <= end
            return (jnp.maximum(pos, stop), tile + jnp.where(used_up, 1, 0), c + jnp.where(used_up, n, 0))

        _, tile, c = lax.while_loop(cond, body, (s, cur_ref[0], cur_ref[1]))
        cur_ref[0] = tile
        cur_ref[1] = c

    @pl.when(j == 0)
    def _():
        xbuf_ref[...] = jnp.zeros_like(xbuf_ref)
        gather(0, 0)

    @pl.when(j + 1 < n_used)
    def _():
        gather(j + 1, (j + 1) & 1)

    @pl.when(j >= n_used)
    def _():
        yb_ref[...] = jnp.zeros_like(yb_ref)

    @pl.when(j < n_used)
    def _():
        _wait_rows(xsl_a_ref, xbuf_ref.at[j & 1], n_valid(j), sem.at[j & 1], MOE_BLOCK)
        changed = jnp.logical_or(j == 0, blk_e_ref[j] != blk_e_ref[jnp.maximum(j - 1, 0)])

        @pl.when(changed)
        def _():
            wg16_ref[...] = wg_ref[0, 0].astype(BF16)
            wu16_ref[...] = wu_ref[0, 0].astype(BF16)
            wd16_ref[...] = wd_ref[0, 0].astype(BF16)

        x = xbuf_ref[j & 1].astype(BF16)
        g = jnp.dot(x, wg16_ref[...], preferred_element_type=F32)
        up = jnp.dot(x, wu16_ref[...], preferred_element_type=F32)
        hid = (g * (0.5 + 0.5 * jnp.tanh(0.5 * g)) * up).astype(BF16)
        yb_ref[...] = jnp.dot(hid, wd16_ref[...], preferred_element_type=F32)


def _experts(layer, xsl_a, xsl_b, blk_e, blk_s, n_used, cnt, meta, w_gate, w_up, w_down, *, ls_a, ls_b):
    d = xsl_a.shape[1]
    de = w_gate.shape[-1]
    n_blk = blk_e.shape[0]

    def w_idx(j, be, bs, nu, ct, mt):
        return (layer, be[jnp.minimum(j, nu[0] - 1)], 0, 0)

    grid_spec = pltpu.PrefetchScalarGridSpec(
        num_scalar_prefetch=5,
        grid=(n_blk,),
        in_specs=[
            pl.BlockSpec(memory_space=pl.ANY),
            pl.BlockSpec(memory_space=pl.ANY),
            pl.BlockSpec((1, 1, d, de), w_idx),
            pl.BlockSpec((1, 1, d, de), w_idx),
            pl.BlockSpec((1, 1, de, d), w_idx),
        ],
        out_specs=pl.BlockSpec((MOE_BLOCK, d), lambda j, *a: (j, 0)),
        scratch_shapes=[
            pltpu.VMEM((2, MOE_BLOCK, d), F32),
            pltpu.VMEM((d, de), BF16),
            pltpu.VMEM((d, de), BF16),
            pltpu.VMEM((de, d), BF16),
            pltpu.SMEM((2,), I32),
            pltpu.SemaphoreType.DMA((2,)),
        ],
    )
    return pl.pallas_call(
        functools.partial(_expert_kernel, tiles_a=xsl_a.shape[0] // ls_a, ls_a=ls_a,
                          tiles_b=xsl_b.shape[0] // ls_b, ls_b=ls_b),
        grid_spec=grid_spec,
        out_shape=jax.ShapeDtypeStruct((n_blk * MOE_BLOCK, d), F32),
        compiler_params=pltpu.CompilerParams(
            dimension_semantics=("arbitrary",), vmem_limit_bytes=VMEM_LIMIT),
        name="moe_experts",
    )(blk_e, blk_s, n_used, cnt, meta, xsl_a, xsl_b, w_gate, w_up, w_down)


def _combine_kernel(cnt_ref, cur0_ref, meta_ref, yb_ref, x1_ref, ri_ref, rf_ref, mod_ref, lng_ref, lnb_ref,
                    o_ref, ybuf_ref, cur_ref, sem, *, alpha, n_t, tile0):
    bt, tl, d = x1_ref.shape
    ls = ybuf_ref.shape[1]
    ls_pow2 = 1 << (ls.bit_length() - 1)
    r = bt * tl
    step = pl.program_id(0) * n_t + pl.program_id(1)
    n_steps = pl.num_programs(0) * n_t

    def gather(i, slot):
        def body(e, carry):
            n = meta_ref[(tile0 + i) * TILE_META + e]
            off = meta_ref[(tile0 + i) * TILE_META + LANES + e]
            _copy_run(yb_ref, ybuf_ref.at[slot], cur_ref[e], off, n, sem.at[slot], r, False)
            cur_ref[e] = cur_ref[e] + n
            return carry

        lax.fori_loop(0, N_EXPERTS, body, 0)

    @pl.when(step == 0)
    def _():
        def init(e, acc):
            cur_ref[e] = acc + cur0_ref[e]
            return acc + ((cnt_ref[e] + MOE_BLOCK - 1) // MOE_BLOCK) * MOE_BLOCK

        lax.fori_loop(0, N_EXPERTS, init, jnp.int32(0))
        ybuf_ref[...] = jnp.zeros_like(ybuf_ref)
        gather(0, 0)

    @pl.when(step + 1 < n_steps)
    def _():
        gather(step + 1, (step + 1) & 1)

    last = (tile0 + step) * TILE_META + N_EXPERTS - 1
    n_rows = meta_ref[last] + meta_ref[last + LANES]
    _wait_rows(yb_ref, ybuf_ref.at[step & 1], n_rows, sem.at[step & 1], ls_pow2)
    ys = ybuf_ref[step & 1].astype(BF16)
    ri = ri_ref[...]
    rf = rf_ref[...]
    slot = lax.broadcasted_iota(I32, (r, ls), 1)
    sel1 = jnp.where(slot == ri[:, 0:1], 1.0, 0.0).astype(BF16)
    sel2 = jnp.where(slot == ri[:, 1:2], 1.0, 0.0).astype(BF16)
    y = (rf[:, 0:1] * jnp.dot(sel1, ys, preferred_element_type=F32)
         + rf[:, 1:2] * jnp.dot(sel2, ys, preferred_element_type=F32))
    gate2 = mod_ref[:, :, 5 * d:6 * d]
    z = alpha * x1_ref[...] + gate2 * y.reshape(bt, tl, d)
    o_ref[...] = _layer_norm(z, lng_ref[0, 1:2, :], lnb_ref[0, 1:2, :])


def _combine(layer, cnt, cur0, meta, yb, x1, ri, rf, mod, ln_g, ln_b, *, bt, tl, ls, tile0, alpha):
    nb, seq, d = x1.shape
    r = bt * tl
    n_t = seq // tl
    grid_spec = pltpu.PrefetchScalarGridSpec(
        num_scalar_prefetch=3,
        grid=(nb // bt, n_t),
        in_specs=[
            pl.BlockSpec(memory_space=pl.ANY),
            pl.BlockSpec((bt, tl, d), lambda b, t, *a: (b, t, 0)),
            pl.BlockSpec((r, LANES), lambda b, t, *a: (b * n_t + t, 0)),
            pl.BlockSpec((r, LANES), lambda b, t, *a: (b * n_t + t, 0)),
            pl.BlockSpec((bt, 1, 6 * d), lambda b, t, *a: (b, 0, 0)),
            pl.BlockSpec((1, 2, d), lambda b, t, *a: (layer, 0, 0)),
            pl.BlockSpec((1, 2, d), lambda b, t, *a: (layer, 0, 0)),
        ],
        out_specs=pl.BlockSpec((bt, tl, d), lambda b, t, *a: (b, t, 0)),
        scratch_shapes=[
            pltpu.VMEM((2, ls, d), F32),
            pltpu.SMEM((N_EXPERTS,), I32),
            pltpu.SemaphoreType.DMA((2,)),
        ],
    )
    return pl.pallas_call(
        functools.partial(_combine_kernel, alpha=alpha, n_t=n_t, tile0=tile0),
        grid_spec=grid_spec,
        out_shape=jax.ShapeDtypeStruct((nb, seq, d), F32),
        compiler_params=pltpu.CompilerParams(
            dimension_semantics=("arbitrary", "arbitrary"), vmem_limit_bytes=VMEM_LIMIT),
        name="moe_combine",
    )(cnt, cur0, meta, yb, x1, ri, rf, mod, ln_g, ln_b)


def _moe_experts(layer, xsl_a, xsl_b, meta_a, meta_b, cnt_a, cnt_b, w_gate, w_up, w_down, *, ls_a, ls_b):
    tiles_a = xsl_a.shape[0] // ls_a
    tiles_b = xsl_b.shape[0] // ls_b
    n_rows = xsl_a.shape[0] + xsl_b.shape[0]
    n_blk = -(-(n_rows + N_EXPERTS * (MOE_BLOCK - 1)) // MOE_BLOCK)
    cnt = cnt_a[0] + cnt_b[0]
    padded = (cnt[:N_EXPERTS] + MOE_BLOCK - 1) // MOE_BLOCK * MOE_BLOCK
    pad_end = jnp.cumsum(padded).astype(I32)
    blk_start = jnp.arange(n_blk, dtype=I32) * MOE_BLOCK
    before = pad_end[None, :] <= blk_start[:, None]
    blk_e = jnp.minimum(jnp.sum(before, axis=1), N_EXPERTS - 1).astype(I32)
    blk_s = blk_start - jnp.sum(jnp.where(before, padded[None, :], 0), axis=1).astype(I32)
    n_used = pad_end[N_EXPERTS - 1:] // MOE_BLOCK
    meta = jnp.concatenate([meta_a.reshape(tiles_a * TILE_META), meta_b.reshape(tiles_b * TILE_META)])
    yb = _experts(layer, xsl_a, xsl_b, blk_e, blk_s, n_used, cnt, meta, w_gate, w_up, w_down,
                  ls_a=ls_a, ls_b=ls_b)
    return yb, cnt, meta


def _slot_capacity(tokens):
    return -(-(2 * tokens + RUN_PAD) // MOE_BLOCK) * MOE_BLOCK


def kernel(x_prompt, x_sample, state_conv, state_hgrn, c_prompt, c_sample, hgrn_lb_logits,
           w_mod, b_mod, w_in, conv_w, gnorm_w, w_out, ln_g, ln_b,
           w_router_group, w_router_expert, w_e_gate, w_e_up, w_e_down):
    depth, d, _ = w_in.shape
    nb_p, seq_p, _ = x_prompt.shape
    nb_s, seq_s, _ = x_sample.shape
    alpha = float((2 * depth) ** 0.25)

    mod_all = _modulation(jnp.concatenate([c_prompt, c_sample], axis=0), w_mod, b_mod)
    mod_p = mod_all[:, :nb_p].reshape(depth, nb_p, 1, 6 * d)
    mod_s = mod_all[:, nb_p:].reshape(depth, nb_s, 1, 6 * d)

    w_in16 = w_in.astype(BF16)
    w_out16 = w_out.astype(BF16)
    gnorm3 = gnorm_w.reshape(depth, 1, HEAD_DIM)
    w_r = jnp.concatenate(
        [w_router_expert, w_router_group,
         jnp.zeros((depth, d, LANES - N_EXPERTS - N_GROUPS), F32)], axis=-1)

    tl_p = min(512, seq_p)
    bt_s = 16
    ls_p = _slot_capacity(tl_p)
    ls_s = _slot_capacity(bt_s * seq_s)
    yp, ys = x_prompt, x_sample
    conv_p, hgrn_p, conv_s, hgrn_s = [], [], [], []
    for l in range(depth):
        x1_p, xsl_p, ri_p, rf_p, meta_p, cb, sb, cnt_p = _mixer(
            l, yp, mod_p[l], hgrn_lb_logits, w_in16, conv_w, gnorm3, w_out16, ln_g, ln_b, w_r,
            None, None, bt=1, tl=tl_p, chunk=64, ls=ls_p)
        x1_s, xsl_s, ri_s, rf_s, meta_s, cs, ss, cnt_s = _mixer(
            l, ys, mod_s[l], hgrn_lb_logits, w_in16, conv_w, gnorm3, w_out16, ln_g, ln_b, w_r,
            state_conv, state_hgrn, bt=bt_s, tl=seq_s, chunk=seq_s, ls=ls_s)
        yb, cnt, meta = _moe_experts(l, xsl_p, xsl_s, meta_p, meta_s, cnt_p, cnt_s,
                                     w_e_gate, w_e_up, w_e_down, ls_a=ls_p, ls_b=ls_s)
        yp = _combine(l, cnt, jnp.zeros_like(cnt), meta, yb, x1_p, ri_p, rf_p, mod_p[l], ln_g, ln_b,
                      bt=1, tl=tl_p, ls=ls_p, tile0=0, alpha=alpha)
        ys = _combine(l, cnt, cnt_p[0], meta, yb, x1_s, ri_s, rf_s, mod_s[l], ln_g, ln_b,
                      bt=bt_s, tl=seq_s, ls=ls_s, tile0=meta_p.shape[0], alpha=alpha)
        conv_p.append(cb)
        hgrn_p.append(sb)
        conv_s.append(cs)
        hgrn_s.append(ss)
    return (yp, ys, jnp.stack(conv_p), jnp.stack(hgrn_p), jnp.stack(conv_s), jnp.stack(hgrn_s))
```

```python
import functools

import jax
import jax.numpy as jnp
from jax import lax
from jax.experimental import pallas as pl
from jax.experimental.pallas import tpu as pltpu

F32 = jnp.float32
BF16 = jnp.bfloat16
I32 = jnp.int32

CONV_W = 3
HEAD_DIM = 128
N_GROUPS = 4
EXPERTS_PER_GROUP = 8
N_EXPERTS = N_GROUPS * EXPERTS_PER_GROUP
LN_EPS = 1e-5
RMS_EPS = 1e-6
LANES = 128
SUBLANES = 8
MOE_BLOCK = 256
RARE_RUN = 64
TILE_META = 2 * LANES
VMEM_LIMIT = 56 * 1024 * 1024
NEG = -1e30
RUN_PAD = N_EXPERTS * (SUBLANES - 1)
LOG2E = 1.4426950408889634


def _seg_row(x, seg, row):
    r, n = x.shape
    x3 = x.reshape(r // seg, seg, n)
    return jnp.broadcast_to(x3[:, row:row + 1, :], x3.shape).reshape(r, n)


def _roll8(x, shift):
    r, n = x.shape
    return pltpu.roll(x.reshape(r // SUBLANES, SUBLANES, n), shift, 1).reshape(r, n)


def _dot_nt(a, b):
    return lax.dot_general(a, b, (((1,), (1,)), ((), ())), preferred_element_type=F32)


def _dot_tn(a, b):
    return lax.dot_general(a, b, (((0,), (0,)), ((), ())), preferred_element_type=F32)


def _split16(x):
    hi = x.astype(BF16)
    return hi, (x - hi.astype(F32)).astype(BF16)


def _layer_norm(z, g, b):
    mu = jnp.mean(z, axis=-1, keepdims=True)
    zc = z - mu
    var = jnp.mean(zc * zc, axis=-1, keepdims=True)
    return zc * lax.rsqrt(var + LN_EPS) * g + b


def _copy_run(src_ref, dst_ref, src0, dst0, n, sem, max_rows, wait):
    def piece(p):
        @pl.when((n & p) != 0)
        def _():
            off = n & (-2 * p)
            cp = pltpu.make_async_copy(
                src_ref.at[pl.ds(pl.multiple_of(src0 + off, SUBLANES), p)],
                dst_ref.at[pl.ds(pl.multiple_of(dst0 + off, SUBLANES), p)], sem)
            if wait:
                cp.wait()
            else:
                cp.start()

    p = max_rows
    if max_rows >= RARE_RUN:
        @pl.when(n >= RARE_RUN)
        def _():
            q = max_rows
            while q >= RARE_RUN:
                piece(q)
                q //= 2
        p = RARE_RUN // 2
    while p >= SUBLANES:
        piece(p)
        p //= 2


def _wait_rows(src_ref, dst_ref, n, sem, max_rows):
    p = max_rows
    while p >= SUBLANES:
        @pl.when((n & p) != 0)
        def _(p=p):
            pltpu.make_async_copy(src_ref.at[pl.ds(0, p)], dst_ref.at[pl.ds(0, p)], sem).wait()
        p //= 2


def _mod_kernel(c_ref, w_ref, b_ref, o_ref):
    c = c_ref[...]
    a = (c * (0.5 + 0.5 * jnp.tanh(0.5 * c))).astype(BF16)
    o_ref[0] = jnp.dot(a, w_ref[0].astype(BF16), preferred_element_type=F32) + b_ref[0]


def _modulation(c_all, w_mod, b_mod):
    depth, d, d6 = w_mod.shape
    nb = c_all.shape[0]
    n_col = d6 // d
    return pl.pallas_call(
        _mod_kernel,
        grid=(depth, n_col),
        in_specs=[
            pl.BlockSpec((nb, d), lambda l, j: (0, 0)),
            pl.BlockSpec((1, d, d), lambda l, j: (l, 0, j)),
            pl.BlockSpec((1, 1, d), lambda l, j: (l, 0, j)),
        ],
        out_specs=pl.BlockSpec((1, nb, d), lambda l, j: (l, 0, j)),
        out_shape=jax.ShapeDtypeStruct((depth, nb, d6), F32),
        compiler_params=pltpu.CompilerParams(
            dimension_semantics=("arbitrary", "arbitrary"), vmem_limit_bytes=VMEM_LIMIT),
        name="modulation",
    )(c_all, w_mod, b_mod.reshape(depth, 1, d6))


def _mixer_kernel(*refs, layer, bt, tl, chunk, ls, n_t, n_tiles, has_state, alpha):
    if has_state:
        (x_ref, mod_ref, xp_ref, modp_ref, lbl_ref, w_in_ref, cw_ref, gn_ref, w_out_ref, lng_ref, lnb_ref,
         wr_ref, conv0_ref, s0_ref,
         x1_ref, xsl_ref, ri_ref, rf_ref, meta_ref, conv_out_ref, s_out_ref, cnt_ref,
         ymix_ref, s_run_ref, tail_ref, cnt_run_ref) = refs
    else:
        (x_ref, mod_ref, xp_ref, modp_ref, lbl_ref, w_in_ref, cw_ref, gn_ref, w_out_ref, lng_ref, lnb_ref,
         wr_ref,
         x1_ref, xsl_ref, ri_ref, rf_ref, meta_ref, conv_out_ref, s_out_ref, cnt_ref,
         ymix_ref, s_run_ref, tail_ref, cnt_run_ref) = refs
    d = x_ref.shape[-1]
    conv_dim = d // 2
    hg = d - conv_dim
    heads = hg // HEAD_DIM
    r = bt * tl
    n_units = r // chunk
    step_id = pl.program_id(0)
    is_real = step_id < n_tiles
    has_prev = step_id > 0

    @pl.when(step_id == 0)
    def _():
        cnt_run_ref[...] = jnp.zeros_like(cnt_run_ref)
        ymix_ref[...] = jnp.zeros_like(ymix_ref)

    if not has_state:
        @pl.when(jnp.logical_and(is_real, step_id % n_t == 0))
        def _():
            s_run_ref[...] = jnp.zeros_like(s_run_ref)
            tail_ref[...] = jnp.zeros_like(tail_ref)

    modp3 = modp_ref[...]
    gate1, shift2, scale2 = (modp3[:, :, k * d:(k + 1) * d] for k in range(2, 5))

    lbl = lbl_ref[...]
    lbe = jnp.exp(lbl - jnp.max(lbl, axis=0, keepdims=True))
    lbp = lbe / jnp.sum(lbe, axis=0, keepdims=True)
    lb_all = lbp[0:1] * 0.0
    for j in range(1, layer + 1):
        lb_all = lb_all + lbp[j:j + 1]

    x3 = x_ref[...]
    mod3 = mod_ref[...]
    shift1, scale1 = (mod3[:, :, k * d:(k + 1) * d] for k in range(2))
    u = (x3 * (1.0 + scale1) + shift1).reshape(r, d).astype(BF16)

    def proj(k, width):
        return jnp.dot(u, w_in_ref[0, :, k:k + width], preferred_element_type=F32)

    conv = {}

    def conv_b():
        conv["b"] = proj(0, conv_dim)

    def conv_c():
        conv["c"] = proj(conv_dim, conv_dim)

    def conv_v():
        v = conv["c"] * proj(2 * conv_dim, conv_dim)
        prev = conv0_ref[0] if has_state else tail_ref[...]
        rows_c = lax.broadcasted_iota(I32, (r, conv_dim), 0)
        pos_t = rows_c & (tl - 1)
        prev1 = jnp.broadcast_to(prev[:, 1:2, :], (bt, tl, conv_dim)).reshape(r, conv_dim)
        prev0 = jnp.broadcast_to(prev[:, 0:1, :], (bt, tl, conv_dim)).reshape(r, conv_dim)
        v1 = jnp.where(pos_t >= 1, pltpu.roll(v, 1, 0), prev1)
        v2 = jnp.where(pos_t >= 2, pltpu.roll(v, 2, 0), jnp.where(pos_t == 1, prev1, prev0))
        cw = cw_ref[0]
        y_conv = conv["b"] * (cw[0:1] * v2 + cw[1:2] * v1 + cw[2:3] * v)
        ymix_ref[:, 0:conv_dim] = y_conv.astype(BF16)
        new_tail = v.reshape(bt, tl, conv_dim)[:, tl - 2:tl, :]
        if not has_state:
            new_tail = jnp.where(is_real, new_tail, prev)
            tail_ref[...] = new_tail
        conv_out_ref[...] = new_tail

    rows = lax.broadcasted_iota(I32, (r, HEAD_DIM), 0)
    pos_c = rows & (chunk - 1)
    pos8 = rows & (SUBLANES - 1)
    gn = gn_ref[0]
    t_i = lax.broadcasted_iota(I32, (chunk, chunk), 0)
    s_i = lax.broadcasted_iota(I32, (chunk, chunk), 1)
    level_masks = []
    m = SUBLANES
    while m < chunk:
        sh = m.bit_length()
        same = (t_i >> sh) == (s_i >> sh)
        level_masks.append((m, jnp.where(same, jnp.where(
            (t_i & (2 * m - 1)) >= m, jnp.where((s_i & (2 * m - 1)) < m, 1.0, 0.0), 0.0), 0.0)))
        m *= 2
    lane = lax.broadcasted_iota(I32, (r, LANES), 1)
    st = {}

    def finish_out_proj():
        st["mix"] = jnp.dot(ymix_ref[...], w_out_ref[0], preferred_element_type=F32).reshape(bt, tl, d)

    def finish_norm():
        x1 = _layer_norm(alpha * xp_ref[...] + gate1 * st["mix"], lng_ref[0, 0:1, :], lnb_ref[0, 0:1, :])
        x1_ref[...] = x1
        u2 = (x1 * (1.0 + scale2) + shift2).reshape(r, d)
        st["u_hi"], st["u_lo"] = _split16(u2)

    def finish_logits():
        w_hi, w_lo = _split16(wr_ref[0])
        st["logits"] = (jnp.dot(st["u_hi"], w_hi, preferred_element_type=F32)
                        + jnp.dot(st["u_lo"], w_hi, preferred_element_type=F32)
                        + jnp.dot(st["u_hi"], w_lo, preferred_element_type=F32))

    def finish_route():
        logits = st["logits"]
        is_g = jnp.logical_and(lane >= N_EXPERTS, lane < N_EXPERTS + N_GROUPS)
        gl = jnp.where(is_g, logits, NEG)
        gm = jnp.max(gl, axis=-1, keepdims=True)
        grp = jnp.min(jnp.where(gl == gm, lane - N_EXPERTS, LANES), axis=-1, keepdims=True)
        p_sel = 1.0 / jnp.sum(jnp.where(is_g, jnp.exp(gl - gm), 0.0), axis=-1, keepdims=True)
        in_grp = jnp.logical_and(lane < N_EXPERTS, (lane >> 3) == grp)
        el = jnp.where(in_grp, logits, NEG)
        m1 = jnp.max(el, axis=-1, keepdims=True)
        i1 = jnp.min(jnp.where(jnp.logical_and(in_grp, el == m1), lane, LANES), axis=-1, keepdims=True)
        in2 = jnp.logical_and(in_grp, lane != i1)
        el2 = jnp.where(in2, logits, NEG)
        m2 = jnp.max(el2, axis=-1, keepdims=True)
        i2 = jnp.min(jnp.where(jnp.logical_and(in2, el2 == m2), lane, LANES), axis=-1, keepdims=True)
        e21 = jnp.exp(m2 - m1)
        rf_ref[...] = jnp.where(lane == 0, p_sel / (1.0 + e21),
                                jnp.where(lane == 1, p_sel * e21 / (1.0 + e21), 0.0))
        st["oh1"] = lane == i1
        st["oh2"] = lane == i2
        ohs = jnp.where(st["oh1"], 1.0, 0.0) + jnp.where(st["oh2"], 1.0, 0.0)
        tr = lax.broadcasted_iota(I32, (r, r), 0)
        tc = lax.broadcasted_iota(I32, (r, r), 1)
        ltri = jnp.where(tr > tc, 1.0, 0.0).astype(BF16)
        st["before"] = jnp.dot(ltri, ohs.astype(BF16), preferred_element_type=F32)
        st["n_tile"] = jnp.sum(ohs, axis=0, keepdims=True)

    def finish_sort():
        n_tile = ((st["n_tile"].astype(I32) + (SUBLANES - 1)) & -SUBLANES).astype(F32)
        ur = lax.broadcasted_iota(I32, (LANES, LANES), 0)
        uc = lax.broadcasted_iota(I32, (LANES, LANES), 1)
        upper_tri = jnp.where(ur < uc, 1.0, 0.0).astype(BF16)
        n_hi, n_lo = _split16(jnp.broadcast_to(n_tile, (SUBLANES, LANES)))
        off_row = (jnp.dot(n_hi, upper_tri, preferred_element_type=F32)
                   + jnp.dot(n_lo, upper_tri, preferred_element_type=F32))[0:1]
        place = st["before"] + off_row
        lpos1 = jnp.sum(jnp.where(st["oh1"], place, 0.0), axis=-1, keepdims=True).astype(I32)
        lpos2 = jnp.sum(jnp.where(st["oh2"], place, 0.0), axis=-1, keepdims=True).astype(I32)
        slot = lax.broadcasted_iota(I32, (r, ls), 1)
        perm_t = jnp.where(jnp.logical_or(slot == lpos1, slot == lpos2), 1.0, 0.0).astype(BF16)
        xsl_ref[...] = _dot_tn(perm_t, st["u_hi"])
        cnt = cnt_run_ref[...] + jnp.where(has_prev, n_tile, 0.0)
        cnt_run_ref[...] = cnt
        cnt_ref[...] = cnt.astype(I32)
        meta_ref[0, 0:1, :] = n_tile.astype(I32)
        meta_ref[0, 1:2, :] = off_row.astype(I32)
        ri_ref[...] = jnp.where(lane == 0, lpos1, jnp.where(lane == 1, lpos2, 0))

    q_off = 3 * conv_dim
    finish_out_proj()
    q_all = proj(q_off, hg)
    finish_norm()
    f_all = proj(q_off + hg, hg)
    i_all = proj(q_off + 2 * hg, hg)
    late = {}

    def proj_g():
        late["g"] = proj(q_off + 3 * hg, hg)

    pending = [(finish_logits, proj_g), (conv_b, finish_route), (conv_c, finish_sort), (conv_v,)]
    for h in range(heads):
        lo = h * HEAD_DIM
        hs = slice(lo, lo + HEAD_DIM)
        lb = lb_all[:, hs]
        qz, fz, vv = q_all[:, hs], f_all[:, hs], i_all[:, hs]
        qq = qz * (0.5 + 0.5 * jnp.tanh(0.5 * qz)) * (HEAD_DIM ** -0.5)
        tf = jnp.tanh(0.5 * fz)
        lf = jnp.log2(lb + (1.0 - lb) * (0.5 + 0.5 * tf))
        kk = (1.0 - lb) * (0.5 - 0.5 * tf)
        b = lf
        step = 1
        while step < chunk:
            b = b + jnp.where(pos_c >= step, pltpu.roll(b, step, 0), 0.0)
            step *= 2
        if h < len(pending):
            for stage in pending[h]:
                stage()
        gz = late["g"][:, hs]
        o = jnp.sum(qq * kk, axis=-1, keepdims=True) * vv
        for dd in range(1, SUBLANES):
            e = qq * _roll8(kk, dd) * jnp.exp2(b - _roll8(b, dd))
            a = jnp.sum(e, axis=-1, keepdims=True)
            o = o + jnp.where(pos8 >= dd, a * _roll8(vv, dd), 0.0)
        vv16 = vv.astype(BF16)
        o_units = [None] * n_units
        if chunk > SUBLANES:
            a_units = [None] * n_units
            for m, mask in level_masks:
                br = _seg_row(b, 2 * m, m - 1)
                y = jnp.exp2(jnp.minimum(jnp.where((rows & (2 * m - 1)) >= m, b - br, br - b), 0.0))
                qm = (qq * y).astype(BF16)
                km = (kk * y).astype(BF16)
                for c in range(n_units):
                    sl = slice(c * chunk, (c + 1) * chunk)
                    part = mask * _dot_nt(qm[sl], km[sl])
                    a_units[c] = part if a_units[c] is None else a_units[c] + part
            for c in range(n_units):
                sl = slice(c * chunk, (c + 1) * chunk)
                o_units[c] = jnp.dot(a_units[c].astype(BF16), vv16[sl], preferred_element_type=F32)
        bend = _seg_row(b, chunk, chunk - 1)
        qt = (qq * jnp.exp2(b)).astype(BF16)
        kh = (kk * jnp.exp2(bend - b)).astype(BF16)
        bl = b.reshape(n_units, chunk, HEAD_DIM)[:, chunk - 1, :]
        bl = jnp.concatenate([bl, jnp.zeros((HEAD_DIM - n_units, HEAD_DIM), F32)], axis=0)
        dec_t = jnp.exp2(bl.T)
        if not has_state:
            s_cur = s_run_ref[h]
        for c in range(n_units):
            sl = slice(c * chunk, (c + 1) * chunk)
            if has_state:
                s_cur = s0_ref[0, c, h]
            o_s = jnp.dot(qt[sl], s_cur.astype(BF16), preferred_element_type=F32)
            o_units[c] = o_s if o_units[c] is None else o_units[c] + o_s
            s_new = dec_t[:, c:c + 1] * s_cur + _dot_tn(kh[sl], vv16[sl])
            if has_state:
                s_out_ref[c, h] = s_new
            else:
                s_cur = s_new
        if not has_state:
            s_cur = jnp.where(is_real, s_cur, s_run_ref[h])
            s_run_ref[h] = s_cur
            s_out_ref[0, h] = s_cur
        o = o + jnp.concatenate(o_units, axis=0)
        o = o * lax.rsqrt(jnp.mean(o * o, axis=-1, keepdims=True) + RMS_EPS)
        o = o * gn * (gz * (0.5 + 0.5 * jnp.tanh(0.5 * gz)))
        ymix_ref[:, conv_dim + lo:conv_dim + lo + HEAD_DIM] = o.astype(BF16)


def _mixer(layer, x, mod, lb_logits, w_in16, conv_w, gnorm3, w_out16, ln_g, ln_b, w_r,
           state_conv, state_hgrn, *, bt, tl, chunk, ls):
    nb, seq, d = x.shape
    depth = w_in16.shape[0]
    conv_dim = d // 2
    hg = d - conv_dim
    heads = hg // HEAD_DIM
    r = bt * tl
    has_state = state_conv is not None
    if has_state:
        assert seq == tl == chunk
    n_t = seq // tl
    n_tiles = (nb // bt) * n_t
    const = dict(pipeline_mode=pl.Buffered(1))

    def now(s):
        return jnp.minimum(s, n_tiles - 1)

    def prev(s):
        return jnp.maximum(s - 1, 0)

    in_specs = [
        pl.BlockSpec((bt, tl, d), lambda s: (now(s) // n_t, now(s) % n_t, 0)),
        pl.BlockSpec((bt, 1, 6 * d), lambda s: (now(s) // n_t, 0, 0)),
        pl.BlockSpec((bt, tl, d), lambda s: (prev(s) // n_t, prev(s) % n_t, 0)),
        pl.BlockSpec((bt, 1, 6 * d), lambda s: (prev(s) // n_t, 0, 0)),
        pl.BlockSpec(lb_logits.shape, lambda s: (0, 0), **const),
        pl.BlockSpec((1, d, w_in16.shape[2]), lambda s: (layer, 0, 0), **const),
        pl.BlockSpec((1, CONV_W, conv_dim), lambda s: (layer, 0, 0), **const),
        pl.BlockSpec((1, 1, HEAD_DIM), lambda s: (layer, 0, 0), **const),
        pl.BlockSpec((1, d, d), lambda s: (layer, 0, 0), **const),
        pl.BlockSpec((1, 2, d), lambda s: (layer, 0, 0), **const),
        pl.BlockSpec((1, 2, d), lambda s: (layer, 0, 0), **const),
        pl.BlockSpec((1, d, LANES), lambda s: (layer, 0, 0), **const),
    ]
    args = [x, mod, x, mod, lb_logits, w_in16, conv_w, gnorm3, w_out16, ln_g, ln_b, w_r]
    if has_state:
        in_specs += [
            pl.BlockSpec((1, bt, CONV_W - 1, conv_dim), lambda s: (layer, now(s), 0, 0)),
            pl.BlockSpec((1, bt, heads, HEAD_DIM, HEAD_DIM), lambda s: (layer, now(s), 0, 0, 0)),
        ]
        args += [state_conv, state_hgrn]
    out_shape = (
        jax.ShapeDtypeStruct((nb, seq, d), F32),
        jax.ShapeDtypeStruct((n_tiles * ls, d), F32),
        jax.ShapeDtypeStruct((nb * seq, LANES), I32),
        jax.ShapeDtypeStruct((nb * seq, LANES), F32),
        jax.ShapeDtypeStruct((n_tiles, 2, LANES), I32),
        jax.ShapeDtypeStruct((nb, CONV_W - 1, conv_dim), F32),
        jax.ShapeDtypeStruct((nb, heads, HEAD_DIM, HEAD_DIM), F32),
        jax.ShapeDtypeStruct((1, LANES), I32),
    )
    out_specs = (
        pl.BlockSpec((bt, tl, d), lambda s: (prev(s) // n_t, prev(s) % n_t, 0)),
        pl.BlockSpec((ls, d), lambda s: (prev(s), 0)),
        pl.BlockSpec((r, LANES), lambda s: (prev(s), 0)),
        pl.BlockSpec((r, LANES), lambda s: (prev(s), 0)),
        pl.BlockSpec((1, 2, LANES), lambda s: (prev(s), 0, 0)),
        pl.BlockSpec((bt, CONV_W - 1, conv_dim), lambda s: (now(s) // n_t, 0, 0)),
        pl.BlockSpec((bt, heads, HEAD_DIM, HEAD_DIM), lambda s: (now(s) // n_t, 0, 0, 0)),
        pl.BlockSpec((1, LANES), lambda s: (0, 0)),
    )
    scratch = [
        pltpu.VMEM((r, d), BF16),
        pltpu.VMEM((heads, HEAD_DIM, HEAD_DIM), F32),
        pltpu.VMEM((1, CONV_W - 1, conv_dim), F32),
        pltpu.VMEM((1, LANES), F32),
    ]
    kern = functools.partial(_mixer_kernel, layer=layer, bt=bt, tl=tl, chunk=chunk, ls=ls, n_t=n_t,
                             n_tiles=n_tiles, has_state=has_state, alpha=float((2 * depth) ** 0.25))
    return pl.pallas_call(
        kern, grid=(n_tiles + 1,), in_specs=in_specs, out_specs=out_specs, out_shape=out_shape,
        scratch_shapes=scratch,
        compiler_params=pltpu.CompilerParams(
            dimension_semantics=("arbitrary",), vmem_limit_bytes=VMEM_LIMIT),
        name="mixer_state" if has_state else "mixer_seq",
    )(*args)


def _expert_kernel(blk_e_ref, blk_s_ref, n_used_ref, cnt_ref, meta_ref,
                   xsl_a_ref, xsl_b_ref, wg_ref, wu_ref, wd_ref, yb_ref,
                   xbuf_ref, wg16_ref, wu16_ref, wd16_ref, cur_ref, sem, *, tiles_a, ls_a, tiles_b, ls_b):
    j = pl.program_id(0)
    n_used = n_used_ref[0]
    n_tiles = tiles_a + tiles_b

    def n_valid(jj):
        return jnp.minimum(MOE_BLOCK, cnt_ref[blk_e_ref[jj]] - blk_s_ref[jj])

    def gather(jj, slot):
        e = blk_e_ref[jj]
        s = blk_s_ref[jj]
        end = s + n_valid(jj)

        @pl.when(s == 0)
        def _():
            cur_ref[0] = 0
            cur_ref[1] = 0

        def cond(st):
            pos, tile, _ = st
            return jnp.logical_and(pos < end, tile < n_tiles)

        def body(st):
            pos, tile, c = st
            n = meta_ref[tile * TILE_META + e]
            off = meta_ref[tile * TILE_META + LANES + e]
            stop = jnp.minimum(c + n, end)
            src = off + (pos - c)

            @pl.when(jnp.logical_and(stop > pos, tile < tiles_a))
            def _():
                _copy_run(xsl_a_ref, xbuf_ref.at[slot], tile * ls_a + src, pos - s, stop - pos,
                          sem.at[slot], MOE_BLOCK, False)

            @pl.when(jnp.logical_and(stop > pos, tile >= tiles_a))
            def _():
                _copy_run(xsl_b_ref, xbuf_ref.at[slot], (tile - tiles_a) * ls_b + src, pos - s, stop - pos,
                          sem.at[slot], MOE_BLOCK, False)

            used_up = (c + n) <= end
            return (jnp.maximum(pos, stop), tile + jnp.where(used_up, 1, 0), c + jnp.where(used_up, n, 0))

        _, tile, c = lax.while_loop(cond, body, (s, cur_ref[0], cur_ref[1]))
        cur_ref[0] = tile
        cur_ref[1] = c

    @pl.when(j == 0)
    def _():
        xbuf_ref[...] = jnp.zeros_like(xbuf_ref)
        gather(0, 0)

    @pl.when(j + 1 < n_used)
    def _():
        gather(j + 1, (j + 1) & 1)

    @pl.when(j >= n_used)
    def _():
        yb_ref[...] = jnp.zeros_like(yb_ref)

    @pl.when(j < n_used)
    def _():
        _wait_rows(xsl_a_ref, xbuf_ref.at[j & 1], n_valid(j), sem.at[j & 1], MOE_BLOCK)
        changed = jnp.logical_or(j == 0, blk_e_ref[j] != blk_e_ref[jnp.maximum(j - 1, 0)])

        @pl.when(changed)
        def _():
            wg16_ref[...] = wg_ref[0, 0].astype(BF16)
            wu16_ref[...] = wu_ref[0, 0].astype(BF16)
            wd16_ref[...] = wd_ref[0, 0].astype(BF16)

        half = MOE_BLOCK // 2
        hidden = []
        for k in range(2):
            x = xbuf_ref[j & 1, k * half:(k + 1) * half, :].astype(BF16)
            g = jnp.dot(x, wg16_ref[...], preferred_element_type=F32)
            up = jnp.dot(x, wu16_ref[...], preferred_element_type=F32)
            hidden.append((g * (0.5 + 0.5 * jnp.tanh(0.5 * g)) * up).astype(BF16))
        for k in range(2):
            yb_ref[k * half:(k + 1) * half, :] = jnp.dot(hidden[k], wd16_ref[...], preferred_element_type=F32)


def _experts(layer, xsl_a, xsl_b, blk_e, blk_s, n_used, cnt, meta, w_gate, w_up, w_down, *, ls_a, ls_b):
    d = xsl_a.shape[1]
    de = w_gate.shape[-1]
    n_blk = blk_e.shape[0]

    def w_idx(j, be, bs, nu, ct, mt):
        return (layer, be[jnp.minimum(j, nu[0] - 1)], 0, 0)

    grid_spec = pltpu.PrefetchScalarGridSpec(
        num_scalar_prefetch=5,
        grid=(n_blk,),
        in_specs=[
            pl.BlockSpec(memory_space=pl.ANY),
            pl.BlockSpec(memory_space=pl.ANY),
            pl.BlockSpec((1, 1, d, de), w_idx),
            pl.BlockSpec((1, 1, d, de), w_idx),
            pl.BlockSpec((1, 1, de, d), w_idx),
        ],
        out_specs=pl.BlockSpec((MOE_BLOCK, d), lambda j, *a: (j, 0)),
        scratch_shapes=[
            pltpu.VMEM((2, MOE_BLOCK, d), F32),
            pltpu.VMEM((d, de), BF16),
            pltpu.VMEM((d, de), BF16),
            pltpu.VMEM((de, d), BF16),
            pltpu.SMEM((2,), I32),
            pltpu.SemaphoreType.DMA((2,)),
        ],
    )
    return pl.pallas_call(
        functools.partial(_expert_kernel, tiles_a=xsl_a.shape[0] // ls_a, ls_a=ls_a,
                          tiles_b=xsl_b.shape[0] // ls_b, ls_b=ls_b),
        grid_spec=grid_spec,
        out_shape=jax.ShapeDtypeStruct((n_blk * MOE_BLOCK, d), F32),
        compiler_params=pltpu.CompilerParams(
            dimension_semantics=("arbitrary",), vmem_limit_bytes=VMEM_LIMIT),
        name="moe_experts",
    )(blk_e, blk_s, n_used, cnt, meta, xsl_a, xsl_b, w_gate, w_up, w_down)


def _combine_kernel(cnt_ref, cur0_ref, meta_ref, yb_ref, x1_ref, ri_ref, rf_ref, mod_ref, lng_ref, lnb_ref,
                    o_ref, ybuf_ref, cur_ref, sem, *, alpha, n_t, tile0):
    bt, tl, d = x1_ref.shape
    ls = ybuf_ref.shape[1]
    ls_pow2 = 1 << (ls.bit_length() - 1)
    r = bt * tl
    step = pl.program_id(0) * n_t + pl.program_id(1)
    n_steps = pl.num_programs(0) * n_t

    def gather(i, slot):
        def body(e, carry):
            n = meta_ref[(tile0 + i) * TILE_META + e]
            off = meta_ref[(tile0 + i) * TILE_META + LANES + e]
            _copy_run(yb_ref, ybuf_ref.at[slot], cur_ref[e], off, n, sem.at[slot], r, False)
            cur_ref[e] = cur_ref[e] + n
            return carry

        lax.fori_loop(0, N_EXPERTS, body, 0)

    @pl.when(step == 0)
    def _():
        def init(e, acc):
            cur_ref[e] = acc + cur0_ref[e]
            return acc + ((cnt_ref[e] + MOE_BLOCK - 1) // MOE_BLOCK) * MOE_BLOCK

        lax.fori_loop(0, N_EXPERTS, init, jnp.int32(0))
        ybuf_ref[...] = jnp.zeros_like(ybuf_ref)
        gather(0, 0)

    @pl.when(step + 1 < n_steps)
    def _():
        gather(step + 1, (step + 1) & 1)

    last = (tile0 + step) * TILE_META + N_EXPERTS - 1
    n_rows = meta_ref[last] + meta_ref[last + LANES]
    _wait_rows(yb_ref, ybuf_ref.at[step & 1], n_rows, sem.at[step & 1], ls_pow2)
    ys = ybuf_ref[step & 1].astype(BF16)
    ri = ri_ref[...]
    rf = rf_ref[...]
    slot = lax.broadcasted_iota(I32, (r, ls), 1)
    sel1 = jnp.where(slot == ri[:, 0:1], 1.0, 0.0).astype(BF16)
    sel2 = jnp.where(slot == ri[:, 1:2], 1.0, 0.0).astype(BF16)
    y = (rf[:, 0:1] * jnp.dot(sel1, ys, preferred_element_type=F32)
         + rf[:, 1:2] * jnp.dot(sel2, ys, preferred_element_type=F32))
    gate2 = mod_ref[:, :, 5 * d:6 * d]
    z = alpha * x1_ref[...] + gate2 * y.reshape(bt, tl, d)
    o_ref[...] = _layer_norm(z, lng_ref[0, 1:2, :], lnb_ref[0, 1:2, :])


def _combine(layer, cnt, cur0, meta, yb, x1, ri, rf, mod, ln_g, ln_b, *, bt, tl, ls, tile0, alpha):
    nb, seq, d = x1.shape
    r = bt * tl
    n_t = seq // tl
    grid_spec = pltpu.PrefetchScalarGridSpec(
        num_scalar_prefetch=3,
        grid=(nb // bt, n_t),
        in_specs=[
            pl.BlockSpec(memory_space=pl.ANY),
            pl.BlockSpec((bt, tl, d), lambda b, t, *a: (b, t, 0)),
            pl.BlockSpec((r, LANES), lambda b, t, *a: (b * n_t + t, 0)),
            pl.BlockSpec((r, LANES), lambda b, t, *a: (b * n_t + t, 0)),
            pl.BlockSpec((bt, 1, 6 * d), lambda b, t, *a: (b, 0, 0)),
            pl.BlockSpec((1, 2, d), lambda b, t, *a: (layer, 0, 0)),
            pl.BlockSpec((1, 2, d), lambda b, t, *a: (layer, 0, 0)),
        ],
        out_specs=pl.BlockSpec((bt, tl, d), lambda b, t, *a: (b, t, 0)),
        scratch_shapes=[
            pltpu.VMEM((2, ls, d), F32),
            pltpu.SMEM((N_EXPERTS,), I32),
            pltpu.SemaphoreType.DMA((2,)),
        ],
    )
    return pl.pallas_call(
        functools.partial(_combine_kernel, alpha=alpha, n_t=n_t, tile0=tile0),
        grid_spec=grid_spec,
        out_shape=jax.ShapeDtypeStruct((nb, seq, d), F32),
        compiler_params=pltpu.CompilerParams(
            dimension_semantics=("arbitrary", "arbitrary"), vmem_limit_bytes=VMEM_LIMIT),
        name="moe_combine",
    )(cnt, cur0, meta, yb, x1, ri, rf, mod, ln_g, ln_b)


def _moe_experts(layer, xsl_a, xsl_b, meta_a, meta_b, cnt_a, cnt_b, w_gate, w_up, w_down, *, ls_a, ls_b):
    tiles_a = xsl_a.shape[0] // ls_a
    tiles_b = xsl_b.shape[0] // ls_b
    n_rows = xsl_a.shape[0] + xsl_b.shape[0]
    n_blk = -(-(n_rows + N_EXPERTS * (MOE_BLOCK - 1)) // MOE_BLOCK)
    cnt = cnt_a[0] + cnt_b[0]
    padded = (cnt[:N_EXPERTS] + MOE_BLOCK - 1) // MOE_BLOCK * MOE_BLOCK
    pad_end = jnp.cumsum(padded).astype(I32)
    blk_start = jnp.arange(n_blk, dtype=I32) * MOE_BLOCK
    before = pad_end[None, :] <= blk_start[:, None]
    blk_e = jnp.minimum(jnp.sum(before, axis=1), N_EXPERTS - 1).astype(I32)
    blk_s = blk_start - jnp.sum(jnp.where(before, padded[None, :], 0), axis=1).astype(I32)
    n_used = pad_end[N_EXPERTS - 1:] // MOE_BLOCK
    meta = jnp.concatenate([meta_a.reshape(tiles_a * TILE_META), meta_b.reshape(tiles_b * TILE_META)])
    yb = _experts(layer, xsl_a, xsl_b, blk_e, blk_s, n_used, cnt, meta, w_gate, w_up, w_down,
                  ls_a=ls_a, ls_b=ls_b)
    return yb, cnt, meta


def _slot_capacity(tokens):
    return -(-(2 * tokens + RUN_PAD) // MOE_BLOCK) * MOE_BLOCK


def kernel(x_prompt, x_sample, state_conv, state_hgrn, c_prompt, c_sample, hgrn_lb_logits,
           w_mod, b_mod, w_in, conv_w, gnorm_w, w_out, ln_g, ln_b,
           w_router_group, w_router_expert, w_e_gate, w_e_up, w_e_down):
    depth, d, _ = w_in.shape
    nb_p, seq_p, _ = x_prompt.shape
    nb_s, seq_s, _ = x_sample.shape
    alpha = float((2 * depth) ** 0.25)

    mod_all = _modulation(jnp.concatenate([c_prompt, c_sample], axis=0), w_mod, b_mod)
    mod_p = mod_all[:, :nb_p].reshape(depth, nb_p, 1, 6 * d)
    mod_s = mod_all[:, nb_p:].reshape(depth, nb_s, 1, 6 * d)

    w_in16 = w_in.astype(BF16)
    w_out16 = w_out.astype(BF16)
    gnorm3 = gnorm_w.reshape(depth, 1, HEAD_DIM)
    w_r = jnp.concatenate(
        [w_router_expert, w_router_group,
         jnp.zeros((depth, d, LANES - N_EXPERTS - N_GROUPS), F32)], axis=-1)

    tl_p = min(512, seq_p)
    bt_s = 16
    ls_p = _slot_capacity(tl_p)
    ls_s = _slot_capacity(bt_s * seq_s)
    yp, ys = x_prompt, x_sample
    conv_p, hgrn_p, conv_s, hgrn_s = [], [], [], []
    for l in range(depth):
        x1_p, xsl_p, ri_p, rf_p, meta_p, cb, sb, cnt_p = _mixer(
            l, yp, mod_p[l], hgrn_lb_logits, w_in16, conv_w, gnorm3, w_out16, ln_g, ln_b, w_r,
            None, None, bt=1, tl=tl_p, chunk=64, ls=ls_p)
        x1_s, xsl_s, ri_s, rf_s, meta_s, cs, ss, cnt_s = _mixer(
            l, ys, mod_s[l], hgrn_lb_logits, w_in16, conv_w, gnorm3, w_out16, ln_g, ln_b, w_r,
            state_conv, state_hgrn, bt=bt_s, tl=seq_s, chunk=seq_s, ls=ls_s)
        yb, cnt, meta = _moe_experts(l, xsl_p, xsl_s, meta_p, meta_s, cnt_p, cnt_s,
                                     w_e_gate, w_e_up, w_e_down, ls_a=ls_p, ls_b=ls_s)
        yp = _combine(l, cnt, jnp.zeros_like(cnt), meta, yb, x1_p, ri_p, rf_p, mod_p[l], ln_g, ln_b,
                      bt=1, tl=tl_p, ls=ls_p, tile0=0, alpha=alpha)
        ys = _combine(l, cnt, cnt_p[0], meta, yb, x1_s, ri_s, rf_s, mod_s[l], ln_g, ln_b,
                      bt=bt_s, tl=seq_s, ls=ls_s, tile0=meta_p.shape[0], alpha=alpha)
        conv_p.append(cb)
        hgrn_p.append(sb)
        conv_s.append(cs)
        hgrn_s.append(ss)
    return (yp, ys, jnp.stack(conv_p), jnp.stack(hgrn_p), jnp.stack(conv_s), jnp.stack(hgrn_s))
```

```python
import functools

import jax
import jax.numpy as jnp
from jax import lax
from jax.experimental import pallas as pl
from jax.experimental.pallas import tpu as pltpu

F32 = jnp.float32
BF16 = jnp.bfloat16
I32 = jnp.int32

CONV_W = 3
HEAD_DIM = 128
N_GROUPS = 4
EXPERTS_PER_GROUP = 8
N_EXPERTS = N_GROUPS * EXPERTS_PER_GROUP
LN_EPS = 1e-5
RMS_EPS = 1e-6
LANES = 128
SUBLANES = 8
MOE_BLOCK = 256
RARE_RUN = 64
TILE_META = 2 * LANES
VMEM_LIMIT = 56 * 1024 * 1024
NEG = -1e30
ROW_ALIGN = 16
RUN_PAD = N_EXPERTS * (ROW_ALIGN - 1)
LOG2E = 1.4426950408889634


def _seg_row(x, seg, row):
    r, n = x.shape
    x3 = x.reshape(r // seg, seg, n)
    return jnp.broadcast_to(x3[:, row:row + 1, :], x3.shape).reshape(r, n)


def _roll8(x, shift):
    r, n = x.shape
    return pltpu.roll(x.reshape(r // SUBLANES, SUBLANES, n), shift, 1).reshape(r, n)


def _dot_nt(a, b):
    return lax.dot_general(a, b, (((1,), (1,)), ((), ())), preferred_element_type=F32)


def _dot_tn(a, b):
    return lax.dot_general(a, b, (((0,), (0,)), ((), ())), preferred_element_type=F32)


def _split16(x):
    hi = x.astype(BF16)
    return hi, (x - hi.astype(F32)).astype(BF16)


def _layer_norm(z, g, b):
    mu = jnp.mean(z, axis=-1, keepdims=True)
    zc = z - mu
    var = jnp.mean(zc * zc, axis=-1, keepdims=True)
    return zc * lax.rsqrt(var + LN_EPS) * g + b


def _copy_run(src_ref, dst_ref, src0, dst0, n, sem, max_rows, wait):
    def piece(p):
        @pl.when((n & p) != 0)
        def _():
            off = n & (-2 * p)
            cp = pltpu.make_async_copy(
                src_ref.at[pl.ds(pl.multiple_of(src0 + off, ROW_ALIGN), p)],
                dst_ref.at[pl.ds(pl.multiple_of(dst0 + off, ROW_ALIGN), p)], sem)
            if wait:
                cp.wait()
            else:
                cp.start()

    p = max_rows
    if max_rows >= RARE_RUN:
        @pl.when(n >= RARE_RUN)
        def _():
            q = max_rows
            while q >= RARE_RUN:
                piece(q)
                q //= 2
        p = RARE_RUN // 2
    while p >= ROW_ALIGN:
        piece(p)
        p //= 2


def _wait_rows(src_ref, dst_ref, n, sem, max_rows):
    p = max_rows
    while p >= ROW_ALIGN:
        @pl.when((n & p) != 0)
        def _(p=p):
            pltpu.make_async_copy(src_ref.at[pl.ds(0, p)], dst_ref.at[pl.ds(0, p)], sem).wait()
        p //= 2


def _mod_kernel(c_ref, w_ref, b_ref, o_ref):
    c = c_ref[...]
    a = (c * (0.5 + 0.5 * jnp.tanh(0.5 * c))).astype(BF16)
    o_ref[0] = jnp.dot(a, w_ref[0].astype(BF16), preferred_element_type=F32) + b_ref[0]


def _modulation(c_all, w_mod, b_mod):
    depth, d, d6 = w_mod.shape
    nb = c_all.shape[0]
    n_col = d6 // d
    return pl.pallas_call(
        _mod_kernel,
        grid=(depth, n_col),
        in_specs=[
            pl.BlockSpec((nb, d), lambda l, j: (0, 0)),
            pl.BlockSpec((1, d, d), lambda l, j: (l, 0, j)),
            pl.BlockSpec((1, 1, d), lambda l, j: (l, 0, j)),
        ],
        out_specs=pl.BlockSpec((1, nb, d), lambda l, j: (l, 0, j)),
        out_shape=jax.ShapeDtypeStruct((depth, nb, d6), F32),
        compiler_params=pltpu.CompilerParams(
            dimension_semantics=("arbitrary", "arbitrary"), vmem_limit_bytes=VMEM_LIMIT),
        name="modulation",
    )(c_all, w_mod, b_mod.reshape(depth, 1, d6))


def _mixer_kernel(*refs, layer, bt, tl, chunk, ls, n_t, n_tiles, has_state, alpha):
    if has_state:
        (x_ref, mod_ref, xp_ref, modp_ref, lbl_ref, w_in_ref, cw_ref, gn_ref, w_out_ref, lng_ref, lnb_ref,
         wr_ref, conv0_ref, s0_ref,
         x1_ref, xsl_ref, ri_ref, rf_ref, meta_ref, conv_out_ref, s_out_ref, cnt_ref,
         ymix_ref, s_run_ref, tail_ref, cnt_run_ref) = refs
    else:
        (x_ref, mod_ref, xp_ref, modp_ref, lbl_ref, w_in_ref, cw_ref, gn_ref, w_out_ref, lng_ref, lnb_ref,
         wr_ref,
         x1_ref, xsl_ref, ri_ref, rf_ref, meta_ref, conv_out_ref, s_out_ref, cnt_ref,
         ymix_ref, s_run_ref, tail_ref, cnt_run_ref) = refs
    d = x_ref.shape[-1]
    conv_dim = d // 2
    hg = d - conv_dim
    heads = hg // HEAD_DIM
    r = bt * tl
    n_units = r // chunk
    step_id = pl.program_id(0)
    is_real = step_id < n_tiles
    has_prev = step_id > 0

    @pl.when(step_id == 0)
    def _():
        cnt_run_ref[...] = jnp.zeros_like(cnt_run_ref)
        ymix_ref[...] = jnp.zeros_like(ymix_ref)

    if not has_state:
        @pl.when(jnp.logical_and(is_real, step_id % n_t == 0))
        def _():
            s_run_ref[...] = jnp.zeros_like(s_run_ref)
            tail_ref[...] = jnp.zeros_like(tail_ref)

    modp3 = modp_ref[...]
    gate1, shift2, scale2 = (modp3[:, :, k * d:(k + 1) * d] for k in range(2, 5))

    lbl = lbl_ref[...]
    lbe = jnp.exp(lbl - jnp.max(lbl, axis=0, keepdims=True))
    lbp = lbe / jnp.sum(lbe, axis=0, keepdims=True)
    lb_all = lbp[0:1] * 0.0
    for j in range(1, layer + 1):
        lb_all = lb_all + lbp[j:j + 1]

    x3 = x_ref[...]
    mod3 = mod_ref[...]
    shift1, scale1 = (mod3[:, :, k * d:(k + 1) * d] for k in range(2))
    u = (x3 * (1.0 + scale1) + shift1).reshape(r, d).astype(BF16)

    def proj(k, width):
        return jnp.dot(u, w_in_ref[0, :, k:k + width], preferred_element_type=F32)

    conv = {}

    def conv_b():
        conv["b"] = proj(0, conv_dim)

    def conv_c():
        conv["c"] = proj(conv_dim, conv_dim)

    def conv_v():
        v = conv["c"] * proj(2 * conv_dim, conv_dim)
        prev = conv0_ref[0] if has_state else tail_ref[...]
        rows_c = lax.broadcasted_iota(I32, (r, conv_dim), 0)
        pos_t = rows_c & (tl - 1)
        prev1 = jnp.broadcast_to(prev[:, 1:2, :], (bt, tl, conv_dim)).reshape(r, conv_dim)
        prev0 = jnp.broadcast_to(prev[:, 0:1, :], (bt, tl, conv_dim)).reshape(r, conv_dim)
        v1 = jnp.where(pos_t >= 1, pltpu.roll(v, 1, 0), prev1)
        v2 = jnp.where(pos_t >= 2, pltpu.roll(v, 2, 0), jnp.where(pos_t == 1, prev1, prev0))
        cw = cw_ref[0]
        y_conv = conv["b"] * (cw[0:1] * v2 + cw[1:2] * v1 + cw[2:3] * v)
        ymix_ref[:, 0:conv_dim] = y_conv.astype(BF16)
        new_tail = v.reshape(bt, tl, conv_dim)[:, tl - 2:tl, :]
        if not has_state:
            new_tail = jnp.where(is_real, new_tail, prev)
            tail_ref[...] = new_tail
        conv_out_ref[...] = new_tail

    rows = lax.broadcasted_iota(I32, (r, HEAD_DIM), 0)
    pos_c = rows & (chunk - 1)
    pos8 = rows & (SUBLANES - 1)
    gn = gn_ref[0]
    t_i = lax.broadcasted_iota(I32, (chunk, chunk), 0)
    s_i = lax.broadcasted_iota(I32, (chunk, chunk), 1)
    level_masks = []
    m = SUBLANES
    while m < chunk:
        sh = m.bit_length()
        same = (t_i >> sh) == (s_i >> sh)
        level_masks.append((m, jnp.where(same, jnp.where(
            (t_i & (2 * m - 1)) >= m, jnp.where((s_i & (2 * m - 1)) < m, 1.0, 0.0), 0.0), 0.0)))
        m *= 2
    lane = lax.broadcasted_iota(I32, (r, LANES), 1)
    st = {}

    def finish_out_proj():
        st["mix"] = jnp.dot(ymix_ref[...], w_out_ref[0], preferred_element_type=F32).reshape(bt, tl, d)

    def finish_norm():
        x1 = _layer_norm(alpha * xp_ref[...] + gate1 * st["mix"], lng_ref[0, 0:1, :], lnb_ref[0, 0:1, :])
        x1_ref[...] = x1
        u2 = (x1 * (1.0 + scale2) + shift2).reshape(r, d)
        st["u_hi"], st["u_lo"] = _split16(u2)

    def finish_logits():
        w_hi, w_lo = _split16(wr_ref[0])
        st["logits"] = (jnp.dot(st["u_hi"], w_hi, preferred_element_type=F32)
                        + jnp.dot(st["u_lo"], w_hi, preferred_element_type=F32)
                        + jnp.dot(st["u_hi"], w_lo, preferred_element_type=F32))

    def finish_route():
        logits = st["logits"]
        is_g = jnp.logical_and(lane >= N_EXPERTS, lane < N_EXPERTS + N_GROUPS)
        gl = jnp.where(is_g, logits, NEG)
        gm = jnp.max(gl, axis=-1, keepdims=True)
        grp = jnp.min(jnp.where(gl == gm, lane - N_EXPERTS, LANES), axis=-1, keepdims=True)
        p_sel = 1.0 / jnp.sum(jnp.where(is_g, jnp.exp(gl - gm), 0.0), axis=-1, keepdims=True)
        in_grp = jnp.logical_and(lane < N_EXPERTS, (lane >> 3) == grp)
        el = jnp.where(in_grp, logits, NEG)
        m1 = jnp.max(el, axis=-1, keepdims=True)
        i1 = jnp.min(jnp.where(jnp.logical_and(in_grp, el == m1), lane, LANES), axis=-1, keepdims=True)
        in2 = jnp.logical_and(in_grp, lane != i1)
        el2 = jnp.where(in2, logits, NEG)
        m2 = jnp.max(el2, axis=-1, keepdims=True)
        i2 = jnp.min(jnp.where(jnp.logical_and(in2, el2 == m2), lane, LANES), axis=-1, keepdims=True)
        e21 = jnp.exp(m2 - m1)
        rf_ref[...] = jnp.where(lane == 0, p_sel / (1.0 + e21),
                                jnp.where(lane == 1, p_sel * e21 / (1.0 + e21), 0.0))
        st["oh1"] = lane == i1
        st["oh2"] = lane == i2
        ohs = jnp.where(st["oh1"], 1.0, 0.0) + jnp.where(st["oh2"], 1.0, 0.0)
        tr = lax.broadcasted_iota(I32, (r, r), 0)
        tc = lax.broadcasted_iota(I32, (r, r), 1)
        ltri = jnp.where(tr > tc, 1.0, 0.0).astype(BF16)
        st["before"] = jnp.dot(ltri, ohs.astype(BF16), preferred_element_type=F32)
        st["n_tile"] = jnp.sum(ohs, axis=0, keepdims=True)

    def finish_sort():
        n_tile = ((st["n_tile"].astype(I32) + (ROW_ALIGN - 1)) & -ROW_ALIGN).astype(F32)
        ur = lax.broadcasted_iota(I32, (LANES, LANES), 0)
        uc = lax.broadcasted_iota(I32, (LANES, LANES), 1)
        upper_tri = jnp.where(ur < uc, 1.0, 0.0).astype(BF16)
        n_hi, n_lo = _split16(jnp.broadcast_to(n_tile, (SUBLANES, LANES)))
        off_row = (jnp.dot(n_hi, upper_tri, preferred_element_type=F32)
                   + jnp.dot(n_lo, upper_tri, preferred_element_type=F32))[0:1]
        place = st["before"] + off_row
        lpos1 = jnp.sum(jnp.where(st["oh1"], place, 0.0), axis=-1, keepdims=True).astype(I32)
        lpos2 = jnp.sum(jnp.where(st["oh2"], place, 0.0), axis=-1, keepdims=True).astype(I32)
        slot = lax.broadcasted_iota(I32, (r, ls), 1)
        perm_t = jnp.where(jnp.logical_or(slot == lpos1, slot == lpos2), 1.0, 0.0).astype(BF16)
        xsl_ref[...] = _dot_tn(perm_t, st["u_hi"]).astype(BF16)
        cnt = cnt_run_ref[...] + jnp.where(has_prev, n_tile, 0.0)
        cnt_run_ref[...] = cnt
        cnt_ref[...] = cnt.astype(I32)
        meta_ref[0, 0:1, :] = n_tile.astype(I32)
        meta_ref[0, 1:2, :] = off_row.astype(I32)
        ri_ref[...] = jnp.where(lane == 0, lpos1, jnp.where(lane == 1, lpos2, 0))

    q_off = 3 * conv_dim
    finish_out_proj()
    q_all = proj(q_off, hg)
    finish_norm()
    f_all = proj(q_off + hg, hg)
    i_all = proj(q_off + 2 * hg, hg)
    late = {}

    def proj_g():
        late["g"] = proj(q_off + 3 * hg, hg)

    pending = [(finish_logits, proj_g), (conv_b, finish_route), (conv_c, finish_sort), (conv_v,)]
    for h in range(heads):
        lo = h * HEAD_DIM
        hs = slice(lo, lo + HEAD_DIM)
        lb = lb_all[:, hs]
        qz, fz, vv = q_all[:, hs], f_all[:, hs], i_all[:, hs]
        qq = qz * (0.5 + 0.5 * jnp.tanh(0.5 * qz)) * (HEAD_DIM ** -0.5)
        tf = jnp.tanh(0.5 * fz)
        lf = jnp.log2(lb + (1.0 - lb) * (0.5 + 0.5 * tf))
        kk = (1.0 - lb) * (0.5 - 0.5 * tf)
        b = lf
        step = 1
        while step < chunk:
            b = b + jnp.where(pos_c >= step, pltpu.roll(b, step, 0), 0.0)
            step *= 2
        if h < len(pending):
            for stage in pending[h]:
                stage()
        gz = late["g"][:, hs]
        o = jnp.sum(qq * kk, axis=-1, keepdims=True) * vv
        for dd in range(1, SUBLANES):
            e = qq * _roll8(kk, dd) * jnp.exp2(b - _roll8(b, dd))
            a = jnp.sum(e, axis=-1, keepdims=True)
            o = o + jnp.where(pos8 >= dd, a * _roll8(vv, dd), 0.0)
        vv16 = vv.astype(BF16)
        o_units = [None] * n_units
        if chunk > SUBLANES:
            a_units = [None] * n_units
            for m, mask in level_masks:
                br = _seg_row(b, 2 * m, m - 1)
                y = jnp.exp2(jnp.minimum(jnp.where((rows & (2 * m - 1)) >= m, b - br, br - b), 0.0))
                qm = (qq * y).astype(BF16)
                km = (kk * y).astype(BF16)
                for c in range(n_units):
                    sl = slice(c * chunk, (c + 1) * chunk)
                    part = mask * _dot_nt(qm[sl], km[sl])
                    a_units[c] = part if a_units[c] is None else a_units[c] + part
            for c in range(n_units):
                sl = slice(c * chunk, (c + 1) * chunk)
                o_units[c] = jnp.dot(a_units[c].astype(BF16), vv16[sl], preferred_element_type=F32)
        bend = _seg_row(b, chunk, chunk - 1)
        qt = (qq * jnp.exp2(b)).astype(BF16)
        kh = (kk * jnp.exp2(bend - b)).astype(BF16)
        bl = b.reshape(n_units, chunk, HEAD_DIM)[:, chunk - 1, :]
        bl = jnp.concatenate([bl, jnp.zeros((HEAD_DIM - n_units, HEAD_DIM), F32)], axis=0)
        dec_t = jnp.exp2(bl.T)
        if not has_state:
            s_cur = s_run_ref[h]
        for c in range(n_units):
            sl = slice(c * chunk, (c + 1) * chunk)
            if has_state:
                s_cur = s0_ref[0, c, h]
            o_s = jnp.dot(qt[sl], s_cur.astype(BF16), preferred_element_type=F32)
            o_units[c] = o_s if o_units[c] is None else o_units[c] + o_s
            s_new = dec_t[:, c:c + 1] * s_cur + _dot_tn(kh[sl], vv16[sl])
            if has_state:
                s_out_ref[c, h] = s_new
            else:
                s_cur = s_new
        if not has_state:
            s_cur = jnp.where(is_real, s_cur, s_run_ref[h])
            s_run_ref[h] = s_cur
            s_out_ref[0, h] = s_cur
        o = o + jnp.concatenate(o_units, axis=0)
        o = o * lax.rsqrt(jnp.mean(o * o, axis=-1, keepdims=True) + RMS_EPS)
        o = o * gn * (gz * (0.5 + 0.5 * jnp.tanh(0.5 * gz)))
        ymix_ref[:, conv_dim + lo:conv_dim + lo + HEAD_DIM] = o.astype(BF16)


def _mixer(layer, x, mod, lb_logits, w_in16, conv_w, gnorm3, w_out16, ln_g, ln_b, w_r,
           state_conv, state_hgrn, *, bt, tl, chunk, ls):
    nb, seq, d = x.shape
    depth = w_in16.shape[0]
    conv_dim = d // 2
    hg = d - conv_dim
    heads = hg // HEAD_DIM
    r = bt * tl
    has_state = state_conv is not None
    if has_state:
        assert seq == tl == chunk
    n_t = seq // tl
    n_tiles = (nb // bt) * n_t
    const = dict(pipeline_mode=pl.Buffered(1))

    def now(s):
        return jnp.minimum(s, n_tiles - 1)

    def prev(s):
        return jnp.maximum(s - 1, 0)

    in_specs = [
        pl.BlockSpec((bt, tl, d), lambda s: (now(s) // n_t, now(s) % n_t, 0)),
        pl.BlockSpec((bt, 1, 6 * d), lambda s: (now(s) // n_t, 0, 0)),
        pl.BlockSpec((bt, tl, d), lambda s: (prev(s) // n_t, prev(s) % n_t, 0)),
        pl.BlockSpec((bt, 1, 6 * d), lambda s: (prev(s) // n_t, 0, 0)),
        pl.BlockSpec(lb_logits.shape, lambda s: (0, 0), **const),
        pl.BlockSpec((1, d, w_in16.shape[2]), lambda s: (layer, 0, 0), **const),
        pl.BlockSpec((1, CONV_W, conv_dim), lambda s: (layer, 0, 0), **const),
        pl.BlockSpec((1, 1, HEAD_DIM), lambda s: (layer, 0, 0), **const),
        pl.BlockSpec((1, d, d), lambda s: (layer, 0, 0), **const),
        pl.BlockSpec((1, 2, d), lambda s: (layer, 0, 0), **const),
        pl.BlockSpec((1, 2, d), lambda s: (layer, 0, 0), **const),
        pl.BlockSpec((1, d, LANES), lambda s: (layer, 0, 0), **const),
    ]
    args = [x, mod, x, mod, lb_logits, w_in16, conv_w, gnorm3, w_out16, ln_g, ln_b, w_r]
    if has_state:
        in_specs += [
            pl.BlockSpec((1, bt, CONV_W - 1, conv_dim), lambda s: (layer, now(s), 0, 0)),
            pl.BlockSpec((1, bt, heads, HEAD_DIM, HEAD_DIM), lambda s: (layer, now(s), 0, 0, 0)),
        ]
        args += [state_conv, state_hgrn]
    out_shape = (
        jax.ShapeDtypeStruct((nb, seq, d), F32),
        jax.ShapeDtypeStruct((n_tiles * ls, d), BF16),
        jax.ShapeDtypeStruct((nb * seq, LANES), I32),
        jax.ShapeDtypeStruct((nb * seq, LANES), F32),
        jax.ShapeDtypeStruct((n_tiles, 2, LANES), I32),
        jax.ShapeDtypeStruct((nb, CONV_W - 1, conv_dim), F32),
        jax.ShapeDtypeStruct((nb, heads, HEAD_DIM, HEAD_DIM), F32),
        jax.ShapeDtypeStruct((1, LANES), I32),
    )
    out_specs = (
        pl.BlockSpec((bt, tl, d), lambda s: (prev(s) // n_t, prev(s) % n_t, 0)),
        pl.BlockSpec((ls, d), lambda s: (prev(s), 0)),
        pl.BlockSpec((r, LANES), lambda s: (prev(s), 0)),
        pl.BlockSpec((r, LANES), lambda s: (prev(s), 0)),
        pl.BlockSpec((1, 2, LANES), lambda s: (prev(s), 0, 0)),
        pl.BlockSpec((bt, CONV_W - 1, conv_dim), lambda s: (now(s) // n_t, 0, 0)),
        pl.BlockSpec((bt, heads, HEAD_DIM, HEAD_DIM), lambda s: (now(s) // n_t, 0, 0, 0)),
        pl.BlockSpec((1, LANES), lambda s: (0, 0)),
    )
    scratch = [
        pltpu.VMEM((r, d), BF16),
        pltpu.VMEM((heads, HEAD_DIM, HEAD_DIM), F32),
        pltpu.VMEM((1, CONV_W - 1, conv_dim), F32),
        pltpu.VMEM((1, LANES), F32),
    ]
    kern = functools.partial(_mixer_kernel, layer=layer, bt=bt, tl=tl, chunk=chunk, ls=ls, n_t=n_t,
                             n_tiles=n_tiles, has_state=has_state, alpha=float((2 * depth) ** 0.25))
    return pl.pallas_call(
        kern, grid=(n_tiles + 1,), in_specs=in_specs, out_specs=out_specs, out_shape=out_shape,
        scratch_shapes=scratch,
        compiler_params=pltpu.CompilerParams(
            dimension_semantics=("arbitrary",), vmem_limit_bytes=VMEM_LIMIT),
        name="mixer_state" if has_state else "mixer_seq",
    )(*args)


def _expert_kernel(blk_e_ref, blk_s_ref, n_used_ref, cnt_ref, meta_ref,
                   xsl_a_ref, xsl_b_ref, wg_ref, wu_ref, wd_ref, yb_ref,
                   xbuf_ref, wg16_ref, wu16_ref, wd16_ref, cur_ref, sem, *, tiles_a, ls_a, tiles_b, ls_b):
    j = pl.program_id(0)
    n_used = n_used_ref[0]
    n_tiles = tiles_a + tiles_b

    def n_valid(jj):
        return jnp.minimum(MOE_BLOCK, cnt_ref[blk_e_ref[jj]] - blk_s_ref[jj])

    def gather(jj, slot):
        e = blk_e_ref[jj]
        s = blk_s_ref[jj]
        end = s + n_valid(jj)

        @pl.when(s == 0)
        def _():
            cur_ref[0] = 0
            cur_ref[1] = 0

        def cond(st):
            pos, tile, _ = st
            return jnp.logical_and(pos < end, tile < n_tiles)

        def body(st):
            pos, tile, c = st
            n = meta_ref[tile * TILE_META + e]
            off = meta_ref[tile * TILE_META + LANES + e]
            stop = jnp.minimum(c + n, end)
            src = off + (pos - c)

            @pl.when(jnp.logical_and(stop > pos, tile < tiles_a))
            def _():
                _copy_run(xsl_a_ref, xbuf_ref.at[slot], tile * ls_a + src, pos - s, stop - pos,
                          sem.at[slot], MOE_BLOCK, False)

            @pl.when(jnp.logical_and(stop > pos, tile >= tiles_a))
            def _():
                _copy_run(xsl_b_ref, xbuf_ref.at[slot], (tile - tiles_a) * ls_b + src, pos - s, stop - pos,
                          sem.at[slot], MOE_BLOCK, False)

            used_up = (c + n) <= end
            return (jnp.maximum(pos, stop), tile + jnp.where(used_up, 1, 0), c + jnp.where(used_up, n, 0))

        _, tile, c = lax.while_loop(cond, body, (s, cur_ref[0], cur_ref[1]))
        cur_ref[0] = tile
        cur_ref[1] = c

    @pl.when(j == 0)
    def _():
        xbuf_ref[...] = jnp.zeros_like(xbuf_ref)
        gather(0, 0)

    @pl.when(j + 1 < n_used)
    def _():
        gather(j + 1, (j + 1) & 1)

    @pl.when(j >= n_used)
    def _():
        yb_ref[...] = jnp.zeros_like(yb_ref)

    @pl.when(j < n_used)
    def _():
        _wait_rows(xsl_a_ref, xbuf_ref.at[j & 1], n_valid(j), sem.at[j & 1], MOE_BLOCK)
        changed = jnp.logical_or(j == 0, blk_e_ref[j] != blk_e_ref[jnp.maximum(j - 1, 0)])

        @pl.when(changed)
        def _():
            wg16_ref[...] = wg_ref[0, 0].astype(BF16)
            wu16_ref[...] = wu_ref[0, 0].astype(BF16)
            wd16_ref[...] = wd_ref[0, 0].astype(BF16)

        half = MOE_BLOCK // 2
        hidden = []
        for k in range(2):
            x = xbuf_ref[j & 1, k * half:(k + 1) * half, :]
            g = jnp.dot(x, wg16_ref[...], preferred_element_type=F32)
            up = jnp.dot(x, wu16_ref[...], preferred_element_type=F32)
            hidden.append((g * (0.5 + 0.5 * jnp.tanh(0.5 * g)) * up).astype(BF16))
        for k in range(2):
            y = jnp.dot(hidden[k], wd16_ref[...], preferred_element_type=F32)
            yb_ref[k * half:(k + 1) * half, :] = y.astype(BF16)


def _experts(layer, xsl_a, xsl_b, blk_e, blk_s, n_used, cnt, meta, w_gate, w_up, w_down, *, ls_a, ls_b):
    d = xsl_a.shape[1]
    de = w_gate.shape[-1]
    n_blk = blk_e.shape[0]

    def w_idx(j, be, bs, nu, ct, mt):
        return (layer, be[jnp.minimum(j, nu[0] - 1)], 0, 0)

    grid_spec = pltpu.PrefetchScalarGridSpec(
        num_scalar_prefetch=5,
        grid=(n_blk,),
        in_specs=[
            pl.BlockSpec(memory_space=pl.ANY),
            pl.BlockSpec(memory_space=pl.ANY),
            pl.BlockSpec((1, 1, d, de), w_idx),
            pl.BlockSpec((1, 1, d, de), w_idx),
            pl.BlockSpec((1, 1, de, d), w_idx),
        ],
        out_specs=pl.BlockSpec((MOE_BLOCK, d), lambda j, *a: (j, 0)),
        scratch_shapes=[
            pltpu.VMEM((2, MOE_BLOCK, d), BF16),
            pltpu.VMEM((d, de), BF16),
            pltpu.VMEM((d, de), BF16),
            pltpu.VMEM((de, d), BF16),
            pltpu.SMEM((2,), I32),
            pltpu.SemaphoreType.DMA((2,)),
        ],
    )
    return pl.pallas_call(
        functools.partial(_expert_kernel, tiles_a=xsl_a.shape[0] // ls_a, ls_a=ls_a,
                          tiles_b=xsl_b.shape[0] // ls_b, ls_b=ls_b),
        grid_spec=grid_spec,
        out_shape=jax.ShapeDtypeStruct((n_blk * MOE_BLOCK, d), BF16),
        compiler_params=pltpu.CompilerParams(
            dimension_semantics=("arbitrary",), vmem_limit_bytes=VMEM_LIMIT),
        name="moe_experts",
    )(blk_e, blk_s, n_used, cnt, meta, xsl_a, xsl_b, w_gate, w_up, w_down)


def _combine_kernel(cnt_ref, cur0_ref, meta_ref, yb_ref, x1_ref, ri_ref, rf_ref, mod_ref, lng_ref, lnb_ref,
                    o_ref, ybuf_ref, cur_ref, sem, *, alpha, n_t, tile0):
    bt, tl, d = x1_ref.shape
    ls = ybuf_ref.shape[1]
    ls_pow2 = 1 << (ls.bit_length() - 1)
    r = bt * tl
    step = pl.program_id(0) * n_t + pl.program_id(1)
    n_steps = pl.num_programs(0) * n_t

    def gather(i, slot):
        def body(e, carry):
            n = meta_ref[(tile0 + i) * TILE_META + e]
            off = meta_ref[(tile0 + i) * TILE_META + LANES + e]
            _copy_run(yb_ref, ybuf_ref.at[slot], cur_ref[e], off, n, sem.at[slot], r, False)
            cur_ref[e] = cur_ref[e] + n
            return carry

        lax.fori_loop(0, N_EXPERTS, body, 0)

    @pl.when(step == 0)
    def _():
        def init(e, acc):
            cur_ref[e] = acc + cur0_ref[e]
            return acc + ((cnt_ref[e] + MOE_BLOCK - 1) // MOE_BLOCK) * MOE_BLOCK

        lax.fori_loop(0, N_EXPERTS, init, jnp.int32(0))
        ybuf_ref[...] = jnp.zeros_like(ybuf_ref)
        gather(0, 0)

    @pl.when(step + 1 < n_steps)
    def _():
        gather(step + 1, (step + 1) & 1)

    last = (tile0 + step) * TILE_META + N_EXPERTS - 1
    n_rows = meta_ref[last] + meta_ref[last + LANES]
    _wait_rows(yb_ref, ybuf_ref.at[step & 1], n_rows, sem.at[step & 1], ls_pow2)
    ys = ybuf_ref[step & 1]
    ri = ri_ref[...]
    rf = rf_ref[...]
    slot = lax.broadcasted_iota(I32, (r, ls), 1)
    sel1 = jnp.where(slot == ri[:, 0:1], 1.0, 0.0).astype(BF16)
    sel2 = jnp.where(slot == ri[:, 1:2], 1.0, 0.0).astype(BF16)
    y = (rf[:, 0:1] * jnp.dot(sel1, ys, preferred_element_type=F32)
         + rf[:, 1:2] * jnp.dot(sel2, ys, preferred_element_type=F32))
    gate2 = mod_ref[:, :, 5 * d:6 * d]
    z = alpha * x1_ref[...] + gate2 * y.reshape(bt, tl, d)
    o_ref[...] = _layer_norm(z, lng_ref[0, 1:2, :], lnb_ref[0, 1:2, :])


def _combine(layer, cnt, cur0, meta, yb, x1, ri, rf, mod, ln_g, ln_b, *, bt, tl, ls, tile0, alpha):
    nb, seq, d = x1.shape
    r = bt * tl
    n_t = seq // tl
    grid_spec = pltpu.PrefetchScalarGridSpec(
        num_scalar_prefetch=3,
        grid=(nb // bt, n_t),
        in_specs=[
            pl.BlockSpec(memory_space=pl.ANY),
            pl.BlockSpec((bt, tl, d), lambda b, t, *a: (b, t, 0)),
            pl.BlockSpec((r, LANES), lambda b, t, *a: (b * n_t + t, 0)),
            pl.BlockSpec((r, LANES), lambda b, t, *a: (b * n_t + t, 0)),
            pl.BlockSpec((bt, 1, 6 * d), lambda b, t, *a: (b, 0, 0)),
            pl.BlockSpec((1, 2, d), lambda b, t, *a: (layer, 0, 0)),
            pl.BlockSpec((1, 2, d), lambda b, t, *a: (layer, 0, 0)),
        ],
        out_specs=pl.BlockSpec((bt, tl, d), lambda b, t, *a: (b, t, 0)),
        scratch_shapes=[
            pltpu.VMEM((2, ls, d), BF16),
            pltpu.SMEM((N_EXPERTS,), I32),
            pltpu.SemaphoreType.DMA((2,)),
        ],
    )
    return pl.pallas_call(
        functools.partial(_combine_kernel, alpha=alpha, n_t=n_t, tile0=tile0),
        grid_spec=grid_spec,
        out_shape=jax.ShapeDtypeStruct((nb, seq, d), F32),
        compiler_params=pltpu.CompilerParams(
            dimension_semantics=("arbitrary", "arbitrary"), vmem_limit_bytes=VMEM_LIMIT),
        name="moe_combine",
    )(cnt, cur0, meta, yb, x1, ri, rf, mod, ln_g, ln_b)


def _moe_experts(layer, xsl_a, xsl_b, meta_a, meta_b, cnt_a, cnt_b, w_gate, w_up, w_down, *, ls_a, ls_b):
    tiles_a = xsl_a.shape[0] // ls_a
    tiles_b = xsl_b.shape[0] // ls_b
    n_rows = xsl_a.shape[0] + xsl_b.shape[0]
    n_blk = -(-(n_rows + N_EXPERTS * (MOE_BLOCK - 1)) // MOE_BLOCK)
    cnt = cnt_a[0] + cnt_b[0]
    padded = (cnt[:N_EXPERTS] + MOE_BLOCK - 1) // MOE_BLOCK * MOE_BLOCK
    pad_end = jnp.cumsum(padded).astype(I32)
    blk_start = jnp.arange(n_blk, dtype=I32) * MOE_BLOCK
    before = pad_end[None, :] <= blk_start[:, None]
    blk_e = jnp.minimum(jnp.sum(before, axis=1), N_EXPERTS - 1).astype(I32)
    blk_s = blk_start - jnp.sum(jnp.where(before, padded[None, :], 0), axis=1).astype(I32)
    n_used = pad_end[N_EXPERTS - 1:] // MOE_BLOCK
    meta = jnp.concatenate([meta_a.reshape(tiles_a * TILE_META), meta_b.reshape(tiles_b * TILE_META)])
    yb = _experts(layer, xsl_a, xsl_b, blk_e, blk_s, n_used, cnt, meta, w_gate, w_up, w_down,
                  ls_a=ls_a, ls_b=ls_b)
    return yb, cnt, meta


def _slot_capacity(tokens):
    return -(-(2 * tokens + RUN_PAD) // MOE_BLOCK) * MOE_BLOCK


def kernel(x_prompt, x_sample, state_conv, state_hgrn, c_prompt, c_sample, hgrn_lb_logits,
           w_mod, b_mod, w_in, conv_w, gnorm_w, w_out, ln_g, ln_b,
           w_router_group, w_router_expert, w_e_gate, w_e_up, w_e_down):
    depth, d, _ = w_in.shape
    nb_p, seq_p, _ = x_prompt.shape
    nb_s, seq_s, _ = x_sample.shape
    alpha = float((2 * depth) ** 0.25)

    mod_all = _modulation(jnp.concatenate([c_prompt, c_sample], axis=0), w_mod, b_mod)
    mod_p = mod_all[:, :nb_p].reshape(depth, nb_p, 1, 6 * d)
    mod_s = mod_all[:, nb_p:].reshape(depth, nb_s, 1, 6 * d)

    w_in16 = w_in.astype(BF16)
    w_out16 = w_out.astype(BF16)
    gnorm3 = gnorm_w.reshape(depth, 1, HEAD_DIM)
    w_r = jnp.concatenate(
        [w_router_expert, w_router_group,
         jnp.zeros((depth, d, LANES - N_EXPERTS - N_GROUPS), F32)], axis=-1)

    tl_p = min(512, seq_p)
    bt_s = 16
    ls_p = _slot_capacity(tl_p)
    ls_s = _slot_capacity(bt_s * seq_s)
    yp, ys = x_prompt, x_sample
    conv_p, hgrn_p, conv_s, hgrn_s = [], [], [], []
    for l in range(depth):
        x1_p, xsl_p, ri_p, rf_p, meta_p, cb, sb, cnt_p = _mixer(
            l, yp, mod_p[l], hgrn_lb_logits, w_in16, conv_w, gnorm3, w_out16, ln_g, ln_b, w_r,
            None, None, bt=1, tl=tl_p, chunk=64, ls=ls_p)
        x1_s, xsl_s, ri_s, rf_s, meta_s, cs, ss, cnt_s = _mixer(
            l, ys, mod_s[l], hgrn_lb_logits, w_in16, conv_w, gnorm3, w_out16, ln_g, ln_b, w_r,
            state_conv, state_hgrn, bt=bt_s, tl=seq_s, chunk=seq_s, ls=ls_s)
        yb, cnt, meta = _moe_experts(l, xsl_p, xsl_s, meta_p, meta_s, cnt_p, cnt_s,
                                     w_e_gate, w_e_up, w_e_down, ls_a=ls_p, ls_b=ls_s)
        yp = _combine(l, cnt, jnp.zeros_like(cnt), meta, yb, x1_p, ri_p, rf_p, mod_p[l], ln_g, ln_b,
                      bt=1, tl=tl_p, ls=ls_p, tile0=0, alpha=alpha)
        ys = _combine(l, cnt, cnt_p[0], meta, yb, x1_s, ri_s, rf_s, mod_s[l], ln_g, ln_b,
                      bt=bt_s, tl=seq_s, ls=ls_s, tile0=meta_p.shape[0], alpha=alpha)
        conv_p.append(cb)
        hgrn_p.append(sb)
        conv_s.append(cs)
        hgrn_s.append(ss)
    return (yp, ys, jnp.stack(conv_p), jnp.stack(hgrn_p), jnp.stack(conv_s), jnp.stack(hgrn_s))
```

```python
import functools

import jax
import jax.numpy as jnp
from jax import lax
from jax.experimental import pallas as pl
from jax.experimental.pallas import tpu as pltpu

F32 = jnp.float32
BF16 = jnp.bfloat16
I32 = jnp.int32

CONV_W = 3
HEAD_DIM = 128
N_GROUPS = 4
EXPERTS_PER_GROUP = 8
N_EXPERTS = N_GROUPS * EXPERTS_PER_GROUP
LN_EPS = 1e-5
RMS_EPS = 1e-6
LANES = 128
SUBLANES = 8
MOE_BLOCK = 256
TILE_META = 2 * LANES
VMEM_LIMIT = 56 * 1024 * 1024
NEG = -1e30
RUN_PAD = N_EXPERTS * (SUBLANES - 1)
LOG2E = 1.4426950408889634


def _seg_row(x, seg, row):
    r, n = x.shape
    x3 = x.reshape(r // seg, seg, n)
    return jnp.broadcast_to(x3[:, row:row + 1, :], x3.shape).reshape(r, n)


def _roll8(x, shift):
    r, n = x.shape
    return pltpu.roll(x.reshape(r // SUBLANES, SUBLANES, n), shift, 1).reshape(r, n)


def _dot_nt(a, b):
    return lax.dot_general(a, b, (((1,), (1,)), ((), ())), preferred_element_type=F32)


def _dot_tn(a, b):
    return lax.dot_general(a, b, (((0,), (0,)), ((), ())), preferred_element_type=F32)


def _split16(x):
    hi = x.astype(BF16)
    return hi, (x - hi.astype(F32)).astype(BF16)


def _layer_norm(z, g, b):
    mu = jnp.mean(z, axis=-1, keepdims=True)
    zc = z - mu
    var = jnp.mean(zc * zc, axis=-1, keepdims=True)
    return zc * lax.rsqrt(var + LN_EPS) * g + b


def _copy_run(src_ref, dst_ref, src0, dst0, n, sem, max_rows, wait):
    off = jnp.int32(0)
    p = max_rows
    while p >= SUBLANES:
        @pl.when((n & p) != 0)
        def _(off=off, p=p):
            cp = pltpu.make_async_copy(
                src_ref.at[pl.ds(pl.multiple_of(src0 + off, SUBLANES), p)],
                dst_ref.at[pl.ds(pl.multiple_of(dst0 + off, SUBLANES), p)], sem)
            if wait:
                cp.wait()
            else:
                cp.start()
        off = off + (n & p)
        p //= 2


def _wait_rows(src_ref, dst_ref, n, sem, max_rows):
    p = max_rows
    while p >= SUBLANES:
        @pl.when((n & p) != 0)
        def _(p=p):
            pltpu.make_async_copy(src_ref.at[pl.ds(0, p)], dst_ref.at[pl.ds(0, p)], sem).wait()
        p //= 2


def _mod_kernel(c_ref, w_ref, b_ref, o_ref):
    c = c_ref[...]
    a = (c * (0.5 + 0.5 * jnp.tanh(0.5 * c))).astype(BF16)
    o_ref[0] = jnp.dot(a, w_ref[0].astype(BF16), preferred_element_type=F32) + b_ref[0]


def _modulation(c_all, w_mod, b_mod):
    depth, d, d6 = w_mod.shape
    nb = c_all.shape[0]
    n_col = d6 // d
    return pl.pallas_call(
        _mod_kernel,
        grid=(depth, n_col),
        in_specs=[
            pl.BlockSpec((nb, d), lambda l, j: (0, 0)),
            pl.BlockSpec((1, d, d), lambda l, j: (l, 0, j)),
            pl.BlockSpec((1, 1, d), lambda l, j: (l, 0, j)),
        ],
        out_specs=pl.BlockSpec((1, nb, d), lambda l, j: (l, 0, j)),
        out_shape=jax.ShapeDtypeStruct((depth, nb, d6), F32),
        compiler_params=pltpu.CompilerParams(
            dimension_semantics=("arbitrary", "arbitrary"), vmem_limit_bytes=VMEM_LIMIT),
        name="modulation",
    )(c_all, w_mod, b_mod.reshape(depth, 1, d6))


def _mixer_kernel(*refs, layer, bt, tl, chunk, ls, n_t, n_tiles, has_state, alpha):
    if has_state:
        (x_ref, mod_ref, xp_ref, modp_ref, lbl_ref, w_in_ref, cw_ref, gn_ref, w_out_ref, lng_ref, lnb_ref,
         wr_ref, conv0_ref, s0_ref,
         x1_ref, xsl_ref, ri_ref, rf_ref, meta_ref, conv_out_ref, s_out_ref, cnt_ref,
         ymix_ref, s_run_ref, tail_ref, cnt_run_ref) = refs
    else:
        (x_ref, mod_ref, xp_ref, modp_ref, lbl_ref, w_in_ref, cw_ref, gn_ref, w_out_ref, lng_ref, lnb_ref,
         wr_ref,
         x1_ref, xsl_ref, ri_ref, rf_ref, meta_ref, conv_out_ref, s_out_ref, cnt_ref,
         ymix_ref, s_run_ref, tail_ref, cnt_run_ref) = refs
    d = x_ref.shape[-1]
    conv_dim = d // 2
    hg = d - conv_dim
    heads = hg // HEAD_DIM
    r = bt * tl
    n_units = r // chunk
    step_id = pl.program_id(0)
    is_real = step_id < n_tiles
    has_prev = step_id > 0

    @pl.when(step_id == 0)
    def _():
        cnt_run_ref[...] = jnp.zeros_like(cnt_run_ref)
        ymix_ref[...] = jnp.zeros_like(ymix_ref)

    if not has_state:
        @pl.when(jnp.logical_and(is_real, step_id % n_t == 0))
        def _():
            s_run_ref[...] = jnp.zeros_like(s_run_ref)
            tail_ref[...] = jnp.zeros_like(tail_ref)

    xp3 = xp_ref[...]
    modp3 = modp_ref[...]
    gate1, shift2, scale2 = (modp3[:, :, k * d:(k + 1) * d] for k in range(2, 5))
    mix = jnp.dot(ymix_ref[...], w_out_ref[0], preferred_element_type=F32).reshape(bt, tl, d)

    lbl = lbl_ref[...]
    lbe = jnp.exp(lbl - jnp.max(lbl, axis=0, keepdims=True))
    lbp = lbe / jnp.sum(lbe, axis=0, keepdims=True)
    lb_all = lbp[0:1] * 0.0
    for j in range(1, layer + 1):
        lb_all = lb_all + lbp[j:j + 1]

    x3 = x_ref[...]
    mod3 = mod_ref[...]
    shift1, scale1 = (mod3[:, :, k * d:(k + 1) * d] for k in range(2))
    u = (x3 * (1.0 + scale1) + shift1).reshape(r, d).astype(BF16)

    def proj(k, width):
        return jnp.dot(u, w_in_ref[0, :, k:k + width], preferred_element_type=F32)

    conv = {}

    def conv_b():
        conv["b"] = proj(0, conv_dim)

    def conv_c():
        conv["c"] = proj(conv_dim, conv_dim)

    def conv_v():
        v = conv["c"] * proj(2 * conv_dim, conv_dim)
        prev = conv0_ref[0] if has_state else tail_ref[...]
        rows_c = lax.broadcasted_iota(I32, (r, conv_dim), 0)
        pos_t = rows_c & (tl - 1)
        prev1 = jnp.broadcast_to(prev[:, 1:2, :], (bt, tl, conv_dim)).reshape(r, conv_dim)
        prev0 = jnp.broadcast_to(prev[:, 0:1, :], (bt, tl, conv_dim)).reshape(r, conv_dim)
        v1 = jnp.where(pos_t >= 1, pltpu.roll(v, 1, 0), prev1)
        v2 = jnp.where(pos_t >= 2, pltpu.roll(v, 2, 0), jnp.where(pos_t == 1, prev1, prev0))
        cw = cw_ref[0]
        y_conv = conv["b"] * (cw[0:1] * v2 + cw[1:2] * v1 + cw[2:3] * v)
        ymix_ref[:, 0:conv_dim] = y_conv.astype(BF16)
        new_tail = v.reshape(bt, tl, conv_dim)[:, tl - 2:tl, :]
        if not has_state:
            new_tail = jnp.where(is_real, new_tail, prev)
            tail_ref[...] = new_tail
        conv_out_ref[...] = new_tail

    rows = lax.broadcasted_iota(I32, (r, HEAD_DIM), 0)
    pos_c = rows & (chunk - 1)
    pos8 = rows & (SUBLANES - 1)
    gn = gn_ref[0]
    t_i = lax.broadcasted_iota(I32, (chunk, chunk), 0)
    s_i = lax.broadcasted_iota(I32, (chunk, chunk), 1)
    level_masks = []
    m = SUBLANES
    while m < chunk:
        sh = m.bit_length()
        same = (t_i >> sh) == (s_i >> sh)
        level_masks.append((m, jnp.where(same, jnp.where(
            (t_i & (2 * m - 1)) >= m, jnp.where((s_i & (2 * m - 1)) < m, 1.0, 0.0), 0.0), 0.0)))
        m *= 2
    lane = lax.broadcasted_iota(I32, (r, LANES), 1)
    st = {}

    def finish_norm():
        x1 = _layer_norm(alpha * xp3 + gate1 * mix, lng_ref[0, 0:1, :], lnb_ref[0, 0:1, :])
        x1_ref[...] = x1
        u2 = (x1 * (1.0 + scale2) + shift2).reshape(r, d)
        st["u_hi"] = u2.astype(BF16)

    def finish_logits():
        st["logits"] = jnp.dot(st["u_hi"], wr_ref[0].astype(BF16), preferred_element_type=F32)

    def finish_route():
        logits = st["logits"]
        is_g = jnp.logical_and(lane >= N_EXPERTS, lane < N_EXPERTS + N_GROUPS)
        gl = jnp.where(is_g, logits, NEG)
        gm = jnp.max(gl, axis=-1, keepdims=True)
        grp = jnp.min(jnp.where(gl == gm, lane - N_EXPERTS, LANES), axis=-1, keepdims=True)
        p_sel = 1.0 / jnp.sum(jnp.where(is_g, jnp.exp(gl - gm), 0.0), axis=-1, keepdims=True)
        in_grp = jnp.logical_and(lane < N_EXPERTS, (lane >> 3) == grp)
        el = jnp.where(in_grp, logits, NEG)
        m1 = jnp.max(el, axis=-1, keepdims=True)
        i1 = jnp.min(jnp.where(jnp.logical_and(in_grp, el == m1), lane, LANES), axis=-1, keepdims=True)
        in2 = jnp.logical_and(in_grp, lane != i1)
        el2 = jnp.where(in2, logits, NEG)
        m2 = jnp.max(el2, axis=-1, keepdims=True)
        i2 = jnp.min(jnp.where(jnp.logical_and(in2, el2 == m2), lane, LANES), axis=-1, keepdims=True)
        e21 = jnp.exp(m2 - m1)
        rf_ref[...] = jnp.where(lane == 0, p_sel / (1.0 + e21),
                                jnp.where(lane == 1, p_sel * e21 / (1.0 + e21), 0.0))
        st["oh1"] = lane == i1
        st["oh2"] = lane == i2
        ohs = jnp.where(st["oh1"], 1.0, 0.0) + jnp.where(st["oh2"], 1.0, 0.0)
        tr = lax.broadcasted_iota(I32, (r, r), 0)
        tc = lax.broadcasted_iota(I32, (r, r), 1)
        ltri = jnp.where(tr > tc, 1.0, 0.0).astype(BF16)
        st["before"] = jnp.dot(ltri, ohs.astype(BF16), preferred_element_type=F32)
        st["n_tile"] = jnp.sum(ohs, axis=0, keepdims=True)

    def finish_sort():
        n_tile = ((st["n_tile"].astype(I32) + (SUBLANES - 1)) & -SUBLANES).astype(F32)
        ur = lax.broadcasted_iota(I32, (LANES, LANES), 0)
        uc = lax.broadcasted_iota(I32, (LANES, LANES), 1)
        upper_tri = jnp.where(ur < uc, 1.0, 0.0).astype(BF16)
        n_hi, n_lo = _split16(jnp.broadcast_to(n_tile, (SUBLANES, LANES)))
        off_row = (jnp.dot(n_hi, upper_tri, preferred_element_type=F32)
                   + jnp.dot(n_lo, upper_tri, preferred_element_type=F32))[0:1]
        place = st["before"] + off_row
        lpos1 = jnp.sum(jnp.where(st["oh1"], place, 0.0), axis=-1, keepdims=True).astype(I32)
        lpos2 = jnp.sum(jnp.where(st["oh2"], place, 0.0), axis=-1, keepdims=True).astype(I32)
        slot = lax.broadcasted_iota(I32, (r, ls), 1)
        perm_t = jnp.where(jnp.logical_or(slot == lpos1, slot == lpos2), 1.0, 0.0).astype(BF16)
        xsl_ref[...] = _dot_tn(perm_t, st["u_hi"])
        cnt = cnt_run_ref[...] + jnp.where(has_prev, n_tile, 0.0)
        cnt_run_ref[...] = cnt
        cnt_ref[...] = cnt.astype(I32)
        meta_ref[0, 0:1, :] = n_tile.astype(I32)
        meta_ref[0, 1:2, :] = off_row.astype(I32)
        ri_ref[...] = jnp.where(lane == 0, lpos1, jnp.where(lane == 1, lpos2, 0))

    q_off = 3 * conv_dim
    q_all = proj(q_off, hg)
    finish_norm()
    f_all = proj(q_off + hg, hg)
    i_all = proj(q_off + 2 * hg, hg)
    late = {}

    def proj_g():
        late["g"] = proj(q_off + 3 * hg, hg)

    pending = [(finish_logits, proj_g), (conv_b, finish_route), (conv_c, finish_sort), (conv_v,)]
    for h in range(heads):
        lo = h * HEAD_DIM
        hs = slice(lo, lo + HEAD_DIM)
        lb = lb_all[:, hs]
        qz, fz, vv = q_all[:, hs], f_all[:, hs], i_all[:, hs]
        qq = qz * (0.5 + 0.5 * jnp.tanh(0.5 * qz)) * (HEAD_DIM ** -0.5)
        tf = jnp.tanh(0.5 * fz)
        lf = jnp.log2(lb + (1.0 - lb) * (0.5 + 0.5 * tf))
        kk = (1.0 - lb) * (0.5 - 0.5 * tf)
        b = lf
        step = 1
        while step < chunk:
            b = b + jnp.where(pos_c >= step, pltpu.roll(b, step, 0), 0.0)
            step *= 2
        if h < len(pending):
            for stage in pending[h]:
                stage()
        gz = late["g"][:, hs]
        o = jnp.sum(qq * kk, axis=-1, keepdims=True) * vv
        for dd in range(1, SUBLANES):
            e = qq * _roll8(kk, dd) * jnp.exp2(b - _roll8(b, dd))
            a = jnp.sum(e, axis=-1, keepdims=True)
            o = o + jnp.where(pos8 >= dd, a * _roll8(vv, dd), 0.0)
        vv16 = vv.astype(BF16)
        o_units = [None] * n_units
        if chunk > SUBLANES:
            a_units = [None] * n_units
            for m, mask in level_masks:
                br = _seg_row(b, 2 * m, m - 1)
                y = jnp.exp2(jnp.minimum(jnp.where((rows & (2 * m - 1)) >= m, b - br, br - b), 0.0))
                qm = (qq * y).astype(BF16)
                km = (kk * y).astype(BF16)
                for c in range(n_units):
                    sl = slice(c * chunk, (c + 1) * chunk)
                    part = mask * _dot_nt(qm[sl], km[sl])
                    a_units[c] = part if a_units[c] is None else a_units[c] + part
            for c in range(n_units):
                sl = slice(c * chunk, (c + 1) * chunk)
                o_units[c] = jnp.dot(a_units[c].astype(BF16), vv16[sl], preferred_element_type=F32)
        bend = _seg_row(b, chunk, chunk - 1)
        qt = (qq * jnp.exp2(b)).astype(BF16)
        kh = (kk * jnp.exp2(bend - b)).astype(BF16)
        bl = b.reshape(n_units, chunk, HEAD_DIM)[:, chunk - 1, :]
        bl = jnp.concatenate([bl, jnp.zeros((HEAD_DIM - n_units, HEAD_DIM), F32)], axis=0)
        dec_t = jnp.exp2(bl.T)
        if not has_state:
            s_cur = s_run_ref[h]
        for c in range(n_units):
            sl = slice(c * chunk, (c + 1) * chunk)
            if has_state:
                s_cur = s0_ref[0, c, h]
            o_s = jnp.dot(qt[sl], s_cur.astype(BF16), preferred_element_type=F32)
            o_units[c] = o_s if o_units[c] is None else o_units[c] + o_s
            s_new = dec_t[:, c:c + 1] * s_cur + _dot_tn(kh[sl], vv16[sl])
            if has_state:
                s_out_ref[c, h] = s_new
            else:
                s_cur = s_new
        if not has_state:
            s_cur = jnp.where(is_real, s_cur, s_run_ref[h])
            s_run_ref[h] = s_cur
            s_out_ref[0, h] = s_cur
        o = o + jnp.concatenate(o_units, axis=0)
        o = o * lax.rsqrt(jnp.mean(o * o, axis=-1, keepdims=True) + RMS_EPS)
        o = o * gn * (gz * (0.5 + 0.5 * jnp.tanh(0.5 * gz)))
        ymix_ref[:, conv_dim + lo:conv_dim + lo + HEAD_DIM] = o.astype(BF16)


def _mixer(layer, x, mod, lb_logits, w_in16, conv_w, gnorm3, w_out16, ln_g, ln_b, w_r,
           state_conv, state_hgrn, *, bt, tl, chunk, ls):
    nb, seq, d = x.shape
    depth = w_in16.shape[0]
    conv_dim = d // 2
    hg = d - conv_dim
    heads = hg // HEAD_DIM
    r = bt * tl
    has_state = state_conv is not None
    if has_state:
        assert seq == tl == chunk
    n_t = seq // tl
    n_tiles = (nb // bt) * n_t
    const = dict(pipeline_mode=pl.Buffered(1))

    def now(s):
        return jnp.minimum(s, n_tiles - 1)

    def prev(s):
        return jnp.maximum(s - 1, 0)

    in_specs = [
        pl.BlockSpec((bt, tl, d), lambda s: (now(s) // n_t, now(s) % n_t, 0)),
        pl.BlockSpec((bt, 1, 6 * d), lambda s: (now(s) // n_t, 0, 0)),
        pl.BlockSpec((bt, tl, d), lambda s: (prev(s) // n_t, prev(s) % n_t, 0)),
        pl.BlockSpec((bt, 1, 6 * d), lambda s: (prev(s) // n_t, 0, 0)),
        pl.BlockSpec(lb_logits.shape, lambda s: (0, 0), **const),
        pl.BlockSpec((1, d, w_in16.shape[2]), lambda s: (layer, 0, 0), **const),
        pl.BlockSpec((1, CONV_W, conv_dim), lambda s: (layer, 0, 0), **const),
        pl.BlockSpec((1, 1, HEAD_DIM), lambda s: (layer, 0, 0), **const),
        pl.BlockSpec((1, d, d), lambda s: (layer, 0, 0), **const),
        pl.BlockSpec((1, 2, d), lambda s: (layer, 0, 0), **const),
        pl.BlockSpec((1, 2, d), lambda s: (layer, 0, 0), **const),
        pl.BlockSpec((1, d, LANES), lambda s: (layer, 0, 0), **const),
    ]
    args = [x, mod, x, mod, lb_logits, w_in16, conv_w, gnorm3, w_out16, ln_g, ln_b, w_r]
    if has_state:
        in_specs += [
            pl.BlockSpec((1, bt, CONV_W - 1, conv_dim), lambda s: (layer, now(s), 0, 0)),
            pl.BlockSpec((1, bt, heads, HEAD_DIM, HEAD_DIM), lambda s: (layer, now(s), 0, 0, 0)),
        ]
        args += [state_conv, state_hgrn]
    out_shape = (
        jax.ShapeDtypeStruct((nb, seq, d), F32),
        jax.ShapeDtypeStruct((n_tiles * ls, d), F32),
        jax.ShapeDtypeStruct((nb * seq, LANES), I32),
        jax.ShapeDtypeStruct((nb * seq, LANES), F32),
        jax.ShapeDtypeStruct((n_tiles, 2, LANES), I32),
        jax.ShapeDtypeStruct((nb, CONV_W - 1, conv_dim), F32),
        jax.ShapeDtypeStruct((nb, heads, HEAD_DIM, HEAD_DIM), F32),
        jax.ShapeDtypeStruct((1, LANES), I32),
    )
    out_specs = (
        pl.BlockSpec((bt, tl, d), lambda s: (prev(s) // n_t, prev(s) % n_t, 0)),
        pl.BlockSpec((ls, d), lambda s: (prev(s), 0)),
        pl.BlockSpec((r, LANES), lambda s: (prev(s), 0)),
        pl.BlockSpec((r, LANES), lambda s: (prev(s), 0)),
        pl.BlockSpec((1, 2, LANES), lambda s: (prev(s), 0, 0)),
        pl.BlockSpec((bt, CONV_W - 1, conv_dim), lambda s: (now(s) // n_t, 0, 0)),
        pl.BlockSpec((bt, heads, HEAD_DIM, HEAD_DIM), lambda s: (now(s) // n_t, 0, 0, 0)),
        pl.BlockSpec((1, LANES), lambda s: (0, 0)),
    )
    scratch = [
        pltpu.VMEM((r, d), BF16),
        pltpu.VMEM((heads, HEAD_DIM, HEAD_DIM), F32),
        pltpu.VMEM((1, CONV_W - 1, conv_dim), F32),
        pltpu.VMEM((1, LANES), F32),
    ]
    kern = functools.partial(_mixer_kernel, layer=layer, bt=bt, tl=tl, chunk=chunk, ls=ls, n_t=n_t,
                             n_tiles=n_tiles, has_state=has_state, alpha=float((2 * depth) ** 0.25))
    return pl.pallas_call(
        kern, grid=(n_tiles + 1,), in_specs=in_specs, out_specs=out_specs, out_shape=out_shape,
        scratch_shapes=scratch,
        compiler_params=pltpu.CompilerParams(
            dimension_semantics=("arbitrary",), vmem_limit_bytes=VMEM_LIMIT),
        name="mixer_state" if has_state else "mixer_seq",
    )(*args)


def _expert_kernel(blk_e_ref, blk_s_ref, n_used_ref, cnt_ref, meta_ref,
                   xsl_a_ref, xsl_b_ref, wg_ref, wu_ref, wd_ref, yb_ref,
                   xbuf_ref, wg16_ref, wu16_ref, wd16_ref, cur_ref, sem, *, tiles_a, ls_a, tiles_b, ls_b):
    j = pl.program_id(0)
    n_used = n_used_ref[0]
    n_tiles = tiles_a + tiles_b

    def n_valid(jj):
        return jnp.minimum(MOE_BLOCK, cnt_ref[blk_e_ref[jj]] - blk_s_ref[jj])

    def gather(jj, slot):
        e = blk_e_ref[jj]
        s = blk_s_ref[jj]
        end = s + n_valid(jj)

        @pl.when(s == 0)
        def _():
            cur_ref[0] = 0
            cur_ref[1] = 0

        def cond(st):
            pos, tile, _ = st
            return jnp.logical_and(pos < end, tile < n_tiles)

        def body(st):
            pos, tile, c = st
            n = meta_ref[tile * TILE_META + e]
            off = meta_ref[tile * TILE_META + LANES + e]
            stop = jnp.minimum(c + n, end)
            src = off + (pos - c)

            @pl.when(jnp.logical_and(stop > pos, tile < tiles_a))
            def _():
                _copy_run(xsl_a_ref, xbuf_ref.at[slot], tile * ls_a + src, pos - s, stop - pos,
                          sem.at[slot], MOE_BLOCK, False)

            @pl.when(jnp.logical_and(stop > pos, tile >= tiles_a))
            def _():
                _copy_run(xsl_b_ref, xbuf_ref.at[slot], (tile - tiles_a) * ls_b + src, pos - s, stop - pos,
                          sem.at[slot], MOE_BLOCK, False)

            used_up = (c + n) <= end
            return (jnp.maximum(pos, stop), tile + jnp.where(used_up, 1, 0), c + jnp.where(used_up, n, 0))

        _, tile, c = lax.while_loop(cond, body, (s, cur_ref[0], cur_ref[1]))
        cur_ref[0] = tile
        cur_ref[1] = c

    @pl.when(j == 0)
    def _():
        xbuf_ref[...] = jnp.zeros_like(xbuf_ref)
        gather(0, 0)

    @pl.when(j + 1 < n_used)
    def _():
        gather(j + 1, (j + 1) & 1)

    @pl.when(j >= n_used)
    def _():
        yb_ref[...] = jnp.zeros_like(yb_ref)

    @pl.when(j < n_used)
    def _():
        _wait_rows(xsl_a_ref, xbuf_ref.at[j & 1], n_valid(j), sem.at[j & 1], MOE_BLOCK)
        changed = jnp.logical_or(j == 0, blk_e_ref[j] != blk_e_ref[jnp.maximum(j - 1, 0)])

        @pl.when(changed)
        def _():
            wg16_ref[...] = wg_ref[0, 0].astype(BF16)
            wu16_ref[...] = wu_ref[0, 0].astype(BF16)
            wd16_ref[...] = wd_ref[0, 0].astype(BF16)

        x = xbuf_ref[j & 1].astype(BF16)
        g = jnp.dot(x, wg16_ref[...], preferred_element_type=F32)
        up = jnp.dot(x, wu16_ref[...], preferred_element_type=F32)
        hid = (g * (0.5 + 0.5 * jnp.tanh(0.5 * g)) * up).astype(BF16)
        yb_ref[...] = jnp.dot(hid, wd16_ref[...], preferred_element_type=F32)


def _experts(layer, xsl_a, xsl_b, blk_e, blk_s, n_used, cnt, meta, w_gate, w_up, w_down, *, ls_a, ls_b):
    d = xsl_a.shape[1]
    de = w_gate.shape[-1]
    n_blk = blk_e.shape[0]

    def w_idx(j, be, bs, nu, ct, mt):
        return (layer, be[jnp.minimum(j, nu[0] - 1)], 0, 0)

    grid_spec = pltpu.PrefetchScalarGridSpec(
        num_scalar_prefetch=5,
        grid=(n_blk,),
        in_specs=[
            pl.BlockSpec(memory_space=pl.ANY),
            pl.BlockSpec(memory_space=pl.ANY),
            pl.BlockSpec((1, 1, d, de), w_idx),
            pl.BlockSpec((1, 1, d, de), w_idx),
            pl.BlockSpec((1, 1, de, d), w_idx),
        ],
        out_specs=pl.BlockSpec((MOE_BLOCK, d), lambda j, *a: (j, 0)),
        scratch_shapes=[
            pltpu.VMEM((2, MOE_BLOCK, d), F32),
            pltpu.VMEM((d, de), BF16),
            pltpu.VMEM((d, de), BF16),
            pltpu.VMEM((de, d), BF16),
            pltpu.SMEM((2,), I32),
            pltpu.SemaphoreType.DMA((2,)),
        ],
    )
    return pl.pallas_call(
        functools.partial(_expert_kernel, tiles_a=xsl_a.shape[0] // ls_a, ls_a=ls_a,
                          tiles_b=xsl_b.shape[0] // ls_b, ls_b=ls_b),
        grid_spec=grid_spec,
        out_shape=jax.ShapeDtypeStruct((n_blk * MOE_BLOCK, d), F32),
        compiler_params=pltpu.CompilerParams(
            dimension_semantics=("arbitrary",), vmem_limit_bytes=VMEM_LIMIT),
        name="moe_experts",
    )(blk_e, blk_s, n_used, cnt, meta, xsl_a, xsl_b, w_gate, w_up, w_down)


def _combine_kernel(cnt_ref, cur0_ref, meta_ref, yb_ref, x1_ref, ri_ref, rf_ref, mod_ref, lng_ref, lnb_ref,
                    o_ref, ybuf_ref, cur_ref, sem, *, alpha, n_t, tile0):
    bt, tl, d = x1_ref.shape
    ls = ybuf_ref.shape[1]
    ls_pow2 = 1 << (ls.bit_length() - 1)
    r = bt * tl
    step = pl.program_id(0) * n_t + pl.program_id(1)
    n_steps = pl.num_programs(0) * n_t

    def gather(i, slot):
        def body(e, carry):
            n = meta_ref[(tile0 + i) * TILE_META + e]
            off = meta_ref[(tile0 + i) * TILE_META + LANES + e]
            _copy_run(yb_ref, ybuf_ref.at[slot], cur_ref[e], off, n, sem.at[slot], r, False)
            cur_ref[e] = cur_ref[e] + n
            return carry

        lax.fori_loop(0, N_EXPERTS, body, 0)

    @pl.when(step == 0)
    def _():
        def init(e, acc):
            cur_ref[e] = acc + cur0_ref[e]
            return acc + ((cnt_ref[e] + MOE_BLOCK - 1) // MOE_BLOCK) * MOE_BLOCK

        lax.fori_loop(0, N_EXPERTS, init, jnp.int32(0))
        ybuf_ref[...] = jnp.zeros_like(ybuf_ref)
        gather(0, 0)

    @pl.when(step + 1 < n_steps)
    def _():
        gather(step + 1, (step + 1) & 1)

    last = (tile0 + step) * TILE_META + N_EXPERTS - 1
    n_rows = meta_ref[last] + meta_ref[last + LANES]
    _wait_rows(yb_ref, ybuf_ref.at[step & 1], n_rows, sem.at[step & 1], ls_pow2)
    ys = ybuf_ref[step & 1].astype(BF16)
    ri = ri_ref[...]
    rf = rf_ref[...]
    slot = lax.broadcasted_iota(I32, (r, ls), 1)
    sel1 = jnp.where(slot == ri[:, 0:1], 1.0, 0.0).astype(BF16)
    sel2 = jnp.where(slot == ri[:, 1:2], 1.0, 0.0).astype(BF16)
    y = (rf[:, 0:1] * jnp.dot(sel1, ys, preferred_element_type=F32)
         + rf[:, 1:2] * jnp.dot(sel2, ys, preferred_element_type=F32))
    gate2 = mod_ref[:, :, 5 * d:6 * d]
    z = alpha * x1_ref[...] + gate2 * y.reshape(bt, tl, d)
    o_ref[...] = _layer_norm(z, lng_ref[0, 1:2, :], lnb_ref[0, 1:2, :])


def _combine(layer, cnt, cur0, meta, yb, x1, ri, rf, mod, ln_g, ln_b, *, bt, tl, ls, tile0, alpha):
    nb, seq, d = x1.shape
    r = bt * tl
    n_t = seq // tl
    grid_spec = pltpu.PrefetchScalarGridSpec(
        num_scalar_prefetch=3,
        grid=(nb // bt, n_t),
        in_specs=[
            pl.BlockSpec(memory_space=pl.ANY),
            pl.BlockSpec((bt, tl, d), lambda b, t, *a: (b, t, 0)),
            pl.BlockSpec((r, LANES), lambda b, t, *a: (b * n_t + t, 0)),
            pl.BlockSpec((r, LANES), lambda b, t, *a: (b * n_t + t, 0)),
            pl.BlockSpec((bt, 1, 6 * d), lambda b, t, *a: (b, 0, 0)),
            pl.BlockSpec((1, 2, d), lambda b, t, *a: (layer, 0, 0)),
            pl.BlockSpec((1, 2, d), lambda b, t, *a: (layer, 0, 0)),
        ],
        out_specs=pl.BlockSpec((bt, tl, d), lambda b, t, *a: (b, t, 0)),
        scratch_shapes=[
            pltpu.VMEM((2, ls, d), F32),
            pltpu.SMEM((N_EXPERTS,), I32),
            pltpu.SemaphoreType.DMA((2,)),
        ],
    )
    return pl.pallas_call(
        functools.partial(_combine_kernel, alpha=alpha, n_t=n_t, tile0=tile0),
        grid_spec=grid_spec,
        out_shape=jax.ShapeDtypeStruct((nb, seq, d), F32),
        compiler_params=pltpu.CompilerParams(
            dimension_semantics=("arbitrary", "arbitrary"), vmem_limit_bytes=VMEM_LIMIT),
        name="moe_combine",
    )(cnt, cur0, meta, yb, x1, ri, rf, mod, ln_g, ln_b)


def _moe_experts(layer, xsl_a, xsl_b, meta_a, meta_b, cnt_a, cnt_b, w_gate, w_up, w_down, *, ls_a, ls_b):
    tiles_a = xsl_a.shape[0] // ls_a
    tiles_b = xsl_b.shape[0] // ls_b
    n_rows = xsl_a.shape[0] + xsl_b.shape[0]
    n_blk = -(-(n_rows + N_EXPERTS * (MOE_BLOCK - 1)) // MOE_BLOCK)
    cnt = cnt_a[0] + cnt_b[0]
    padded = (cnt[:N_EXPERTS] + MOE_BLOCK - 1) // MOE_BLOCK * MOE_BLOCK
    pad_end = jnp.cumsum(padded).astype(I32)
    blk_start = jnp.arange(n_blk, dtype=I32) * MOE_BLOCK
    before = pad_end[None, :] <= blk_start[:, None]
    blk_e = jnp.minimum(jnp.sum(before, axis=1), N_EXPERTS - 1).astype(I32)
    blk_s = blk_start - jnp.sum(jnp.where(before, padded[None, :], 0), axis=1).astype(I32)
    n_used = pad_end[N_EXPERTS - 1:] // MOE_BLOCK
    meta = jnp.concatenate([meta_a.reshape(tiles_a * TILE_META), meta_b.reshape(tiles_b * TILE_META)])
    yb = _experts(layer, xsl_a, xsl_b, blk_e, blk_s, n_used, cnt, meta, w_gate, w_up, w_down,
                  ls_a=ls_a, ls_b=ls_b)
    return yb, cnt, meta


def _slot_capacity(tokens):
    return -(-(2 * tokens + RUN_PAD) // MOE_BLOCK) * MOE_BLOCK


def kernel(x_prompt, x_sample, state_conv, state_hgrn, c_prompt, c_sample, hgrn_lb_logits,
           w_mod, b_mod, w_in, conv_w, gnorm_w, w_out, ln_g, ln_b,
           w_router_group, w_router_expert, w_e_gate, w_e_up, w_e_down):
    depth, d, _ = w_in.shape
    nb_p, seq_p, _ = x_prompt.shape
    nb_s, seq_s, _ = x_sample.shape
    alpha = float((2 * depth) ** 0.25)

    mod_all = _modulation(jnp.concatenate([c_prompt, c_sample], axis=0), w_mod, b_mod)
    mod_p = mod_all[:, :nb_p].reshape(depth, nb_p, 1, 6 * d)
    mod_s = mod_all[:, nb_p:].reshape(depth, nb_s, 1, 6 * d)

    w_in16 = w_in.astype(BF16)
    w_out16 = w_out.astype(BF16)
    gnorm3 = gnorm_w.reshape(depth, 1, HEAD_DIM)
    w_r = jnp.concatenate(
        [w_router_expert, w_router_group,
         jnp.zeros((depth, d, LANES - N_EXPERTS - N_GROUPS), F32)], axis=-1)

    tl_p = min(512, seq_p)
    bt_s = 16
    ls_p = _slot_capacity(tl_p)
    ls_s = _slot_capacity(bt_s * seq_s)
    yp, ys = x_prompt, x_sample
    conv_p, hgrn_p, conv_s, hgrn_s = [], [], [], []
    for l in range(depth):
        x1_p, xsl_p, ri_p, rf_p, meta_p, cb, sb, cnt_p = _mixer(
            l, yp, mod_p[l], hgrn_lb_logits, w_in16, conv_w, gnorm3, w_out16, ln_g, ln_b, w_r,
            None, None, bt=1, tl=tl_p, chunk=64, ls=ls_p)
        x1_s, xsl_s, ri_s, rf_s, meta_s, cs, ss, cnt_s = _mixer(
            l, ys, mod_s[l], hgrn_lb_logits, w_in16, conv_w, gnorm3, w_out16, ln_g, ln_b, w_r,
            state_conv, state_hgrn, bt=bt_s, tl=seq_s, chunk=seq_s, ls=ls_s)
        yb, cnt, meta = _moe_experts(l, xsl_p, xsl_s, meta_p, meta_s, cnt_p, cnt_s,
                                     w_e_gate, w_e_up, w_e_down, ls_a=ls_p, ls_b=ls_s)
        yp = _combine(l, cnt, jnp.zeros_like(cnt), meta, yb, x1_p, ri_p, rf_p, mod_p[l], ln_g, ln_b,
                      bt=1, tl=tl_p, ls=ls_p, tile0=0, alpha=alpha)
        ys = _combine(l, cnt, cnt_p[0], meta, yb, x1_s, ri_s, rf_s, mod_s[l], ln_g, ln_b,
                      bt=bt_s, tl=seq_s, ls=ls_s, tile0=meta_p.shape[0], alpha=alpha)
        conv_p.append(cb)
        hgrn_p.append(sb)
        conv_s.append(cs)
        hgrn_s.append(ss)
    return (yp, ys, jnp.stack(conv_p), jnp.stack(hgrn_p), jnp.stack(conv_s), jnp.stack(hgrn_s))
```
